```python
import math
import jax, jax.numpy as jnp
from jax import lax
import numpy as np

D_MODEL = 2048
BATCH = 2
SEQ = 8192
DEPTH = 4

N_META = 16
MIX_WIDTH = D_MODEL
N_EVEN = (DEPTH + 1) // 2
N_ODD = DEPTH // 2
NORM_EPS = 1e-6
LRU_WIDTH = MIX_WIDTH // 2
LRU_BLOCKS = 8
LRU_BLOCK = LRU_WIDTH // LRU_BLOCKS
CONV_WIDTH = 4
LRU_C = 8.0
DA_WIDTH = MIX_WIDTH - LRU_WIDTH
DA_HEADS = 8
DA_HEAD_DIM = DA_WIDTH // DA_HEADS
DA_QK_DIM = DA_HEAD_DIM // 2
Q_BLOCK = 128
EVEN_IN = 2 * LRU_WIDTH + 3 * DA_WIDTH
S5_WIDTH = MIX_WIDTH // 2
S5_GROUP = 16
S5_GROUPS = S5_WIDTH // S5_GROUP
S5_STATE = 64
RW_WIDTH = MIX_WIDTH - S5_WIDTH
RW_HEAD_DIM = 64
RW_HEADS = RW_WIDTH // RW_HEAD_DIM
RW_DECAY_LORA = max(32, int(round(1.8 * D_MODEL ** 0.5 / 32)) * 32)
RW_AAA_LORA = RW_DECAY_LORA
RW_GATE_LORA = max(32, int(round(0.6 * D_MODEL ** 0.8 / 32)) * 32)
RW_IN = 3 * RW_WIDTH + RW_DECAY_LORA + RW_AAA_LORA + RW_GATE_LORA
RW_GN_EPS = 64e-5
ODD_IN = S5_WIDTH + RW_IN
FFN_HIDDEN = ((-(-8 * D_MODEL // 3) + 255) // 256) * 256

kernel_name = 'hybrid_lru_diffattn_s5_rwkv7_trunk'


def rmsnorm(x, g, eps=NORM_EPS):
    x32 = x.astype(jnp.float32)
    y = x32 * lax.rsqrt(jnp.mean(x32 * x32, axis=-1, keepdims=True) + eps)
    return (y * g.astype(jnp.float32)).astype(x.dtype)


def split_cols(z, sizes):
    out, start = [], 0
    for s in sizes:
        out.append(z[..., start:start + s])
        start += s
    return out


def causal_depthwise_conv(u, w, b):
    T = u.shape[1]
    up = jnp.pad(u, ((0, 0), (CONV_WIDTH - 1, 0), (0, 0)))
    out = b
    for j in range(CONV_WIDTH):
        out = out + up[:, j:j + T] * w[j]
    return out


def linear_scan(a, b):
    def combine(l, r):
        al, bl = l
        ar, br = r
        return ar * al, ar * bl + br
    _, h = lax.associative_scan(combine, (a, b), axis=1)
    return h


def rg_lru(u, w_a, b_a, w_x, b_x, lam):
    Bb, T, W = u.shape
    u = u.astype(jnp.float32)
    ub = u.reshape(Bb, T, LRU_BLOCKS, LRU_BLOCK)
    def gate(w, b):
        z = jnp.einsum('btnc,ncd->btnd', ub, w.astype(jnp.float32)).reshape(Bb, T, W)
        return jax.nn.sigmoid(z + b.astype(jnp.float32))
    r = gate(w_a, b_a)
    i = gate(w_x, b_x)
    log_a = -LRU_C * r * jax.nn.softplus(-lam.astype(jnp.float32))
    a = jnp.exp(log_a)
    b = jnp.sqrt(-jnp.expm1(2.0 * log_a)) * (i * u)
    return linear_scan(a, b)


def diff_attention(q, k, v, lq1, lk1, lq2, lk2, subln_g, lambda_init):
    Bb, T, _ = q.shape
    H, d, E = DA_HEADS, DA_QK_DIM, DA_HEAD_DIM
    f32 = jnp.float32
    q = q.astype(f32).reshape(Bb, T, H, 2, d) * (d ** -0.5)
    k = k.astype(f32).reshape(Bb, T, H, 2, d)
    v = v.astype(f32).reshape(Bb, T, H, E)
    lam = (jnp.exp(jnp.sum(lq1.astype(f32) * lk1.astype(f32)))
           - jnp.exp(jnp.sum(lq2.astype(f32) * lk2.astype(f32))) + lambda_init)
    slopes = 2.0 ** (-8.0 * jnp.arange(1, H + 1, dtype=f32) / H)
    k_pos = jnp.arange(T, dtype=jnp.int32)

    def attend(args):
        qb, q_pos = args
        s = jnp.einsum('bqhmd,bkhmd->mbhqk', qb, k)
        dist = (q_pos[:, None] - k_pos[None, :]).astype(f32)
        s = s - slopes[:, None, None] * dist
        s = jnp.where(k_pos[None, :] <= q_pos[:, None], s, -jnp.inf)
        p = jax.nn.softmax(s, axis=-1)
        attn = p[0] - lam * p[1]
        return jnp.einsum('bhqk,bkhe->bqhe', attn, v)

    o_meta = attend((q[:, :N_META], jnp.arange(N_META, dtype=jnp.int32)))
    n_blk = (T - N_META) // Q_BLOCK
    q_real = q[:, N_META:].reshape(Bb, n_blk, Q_BLOCK, H, 2, d).transpose(1, 0, 2, 3, 4, 5)
    pos = (N_META + jnp.arange(T - N_META, dtype=jnp.int32)).reshape(n_blk, Q_BLOCK)
    o_real = lax.map(attend, (q_real, pos))
    o_real = o_real.transpose(1, 0, 2, 3, 4).reshape(Bb, T - N_META, H, E)
    o = jnp.concatenate([o_meta, o_real], axis=1)
    o = rmsnorm(o, subln_g, 1e-5) * (1.0 - lambda_init)
    return o.reshape(Bb, T, H * E)


def s5_layer(u, lam_re, lam_im, log_dt, b_re, b_im, c_re, c_im, d_skip):
    Bb, T, W = u.shape
    f32 = jnp.float32
    u = u.astype(f32)
    lr, li = lam_re.astype(f32), lam_im.astype(f32)
    dt = jnp.exp(log_dt.astype(f32))[:, None]
    mag = jnp.exp(lr * dt)
    ar, ai = mag * jnp.cos(li * dt), mag * jnp.sin(li * dt)
    den = lr * lr + li * li
    cr = ((ar - 1.0) * lr + ai * li) / den
    ci = (ai * lr - (ar - 1.0) * li) / den
    br, bi = b_re.astype(f32), b_im.astype(f32)
    bbr = cr[..., None] * br - ci[..., None] * bi
    bbi = cr[..., None] * bi + ci[..., None] * br
    ug = u.reshape(Bb, T, S5_GROUPS, S5_GROUP)
    bu_r = jnp.einsum('btgc,gpc->btgp', ug, bbr)
    bu_i = jnp.einsum('btgc,gpc->btgp', ug, bbi)
    a_r = jnp.broadcast_to(ar, bu_r.shape)
    a_i = jnp.broadcast_to(ai, bu_r.shape)

    def combine(l, r):
        alr, ali, blr, bli = l
        arr, ari, brr, bri = r
        return (arr * alr - ari * ali, arr * ali + ari * alr,
                arr * blr - ari * bli + brr, arr * bli + ari * blr + bri)

    _, _, hr, hi = lax.associative_scan(combine, (a_r, a_i, bu_r, bu_i), axis=1)
    y = (jnp.einsum('btgp,gcp->btgc', hr, c_re.astype(f32))
         - jnp.einsum('btgp,gcp->btgc', hi, c_im.astype(f32)))
    return y.reshape(Bb, T, W) + d_skip.astype(f32) * u


def rwkv7_time_mix(dcols, mu, w0, w2, a0, a2, g2, k_k, k_a, r_k, ln_w, ln_b):
    Bb, T, _ = dcols.shape
    H, N = RW_HEADS, RW_HEAD_DIM
    f32 = jnp.float32
    dcols = dcols.astype(f32)
    d_prev = jnp.pad(dcols, ((0, 0), (1, 0), (0, 0)))[:, :-1]
    dcols = dcols + (d_prev - dcols) * mu.astype(f32)
    r, k, v, wl, al, gl = split_cols(dcols, (RW_WIDTH, RW_WIDTH, RW_WIDTH, RW_DECAY_LORA, RW_AAA_LORA, RW_GATE_LORA))
    w = -jax.nn.softplus(-(w0.astype(f32) + jnp.tanh(wl) @ w2.astype(f32))) - 0.5
    decay = jnp.exp(-jnp.exp(w))
    a = jax.nn.sigmoid(a0.astype(f32) + al @ a2.astype(f32))
    g = jax.nn.sigmoid(gl) @ g2.astype(f32)
    heads = lambda t: t.reshape(Bb, T, H, N)
    kk = heads(k * k_k.astype(f32))
    kk = kk / jnp.maximum(jnp.sqrt(jnp.sum(kk * kk, axis=-1, keepdims=True)), 1e-12)
    k = k * (1.0 + (a - 1.0) * k_a.astype(f32))

    def step(S, inp):
        r_t, w_t, k_t, v_t, kk_t, a_t = inp
        sa = jnp.einsum('bhij,bhj->bhi', S, -kk_t)
        S = (S * w_t[:, :, None, :] + sa[..., None] * (kk_t * a_t)[:, :, None, :]
             + v_t[..., None] * k_t[:, :, None, :])
        return S, jnp.einsum('bhij,bhj->bhi', S, r_t)

    xs = (heads(r), heads(decay), heads(k), heads(v), kk, heads(a))
    xs = tuple(jnp.moveaxis(t, 1, 0) for t in xs)
    S0 = jnp.zeros((Bb, H, N, N), f32)
    _, y = lax.scan(step, S0, xs)
    y = jnp.moveaxis(y, 0, 1)
    mean = jnp.mean(y, axis=-1, keepdims=True)
    var = jnp.mean(jnp.square(y - mean), axis=-1, keepdims=True)
    y = ((y - mean) * lax.rsqrt(var + RW_GN_EPS)).reshape(Bb, T, RW_WIDTH)
    y = y * ln_w.astype(f32) + ln_b.astype(f32)
    bonus = jnp.sum(heads(r * k * r_k.astype(f32)), axis=-1, keepdims=True) * heads(v)
    return (y + bonus.reshape(Bb, T, RW_WIDTH)) * g


def swiglu(h, w_gate, w_up, w_down):
    return (jax.nn.silu(h @ w_gate) * (h @ w_up)) @ w_down


def setup_inputs(seed: int = 0) -> dict:
    key = jax.random.key(seed)
    ks = iter(jax.random.split(key, 64))
    f32 = jnp.float32
    def nrm(shape, scale):
        return jax.random.normal(next(ks), shape, f32) * scale
    def unif(shape, lo, hi):
        return jax.random.uniform(next(ks), shape, f32, lo, hi)
    NE, NO = N_EVEN, N_ODD
    lru_s = unif((NE, LRU_WIDTH), 0.9, 0.999) ** (1.0 / LRU_C)
    lam_im = jnp.pi * jnp.arange(S5_STATE, dtype=f32) + nrm((NO, S5_GROUPS, S5_STATE), 0.01)
    return {
        'x': nrm((BATCH, SEQ, D_MODEL), 1.0),
        'meta_tokens': nrm((N_META, D_MODEL), 1.0),
        'norm_mix_g': 1.0 + nrm((DEPTH, D_MODEL), 0.02),
        'norm_ffn_g': 1.0 + nrm((DEPTH, D_MODEL), 0.02),
        'final_norm_g': 1.0 + nrm((D_MODEL,), 0.02),
        'ev_w_in': nrm((NE, D_MODEL, EVEN_IN), D_MODEL ** -0.5),
        'ev_conv_w': nrm((NE, CONV_WIDTH, LRU_WIDTH), CONV_WIDTH ** -0.5),
        'ev_conv_b': nrm((NE, LRU_WIDTH), 0.02),
        'ev_lru_wa': nrm((NE, LRU_BLOCKS, LRU_BLOCK, LRU_BLOCK), LRU_BLOCK ** -0.5),
        'ev_lru_ba': nrm((NE, LRU_WIDTH), 0.02),
        'ev_lru_wx': nrm((NE, LRU_BLOCKS, LRU_BLOCK, LRU_BLOCK), LRU_BLOCK ** -0.5),
        'ev_lru_bx': nrm((NE, LRU_WIDTH), 0.02),
        'ev_lru_lambda': jnp.log(lru_s) - jnp.log1p(-lru_s),
        'ev_lq1': nrm((NE, DA_QK_DIM), 0.1),
        'ev_lk1': nrm((NE, DA_QK_DIM), 0.1),
        'ev_lq2': nrm((NE, DA_QK_DIM), 0.1),
        'ev_lk2': nrm((NE, DA_QK_DIM), 0.1),
        'ev_subln_g': 1.0 + nrm((NE, DA_HEAD_DIM), 0.02),
        'ev_w_out': nrm((NE, MIX_WIDTH, D_MODEL), MIX_WIDTH ** -0.5),
        'od_w_in': nrm((NO, D_MODEL, ODD_IN), D_MODEL ** -0.5),
        'od_s5_lam_re': -0.5 + nrm((NO, S5_GROUPS, S5_STATE), 0.01),
        'od_s5_lam_im': lam_im,
        'od_s5_log_dt': unif((NO, S5_GROUPS), math.log(0.001), math.log(0.1)),
        'od_s5_b_re': nrm((NO, S5_GROUPS, S5_STATE, S5_GROUP), (2 * S5_GROUP) ** -0.5),
        'od_s5_b_im': nrm((NO, S5_GROUPS, S5_STATE, S5_GROUP), (2 * S5_GROUP) ** -0.5),
        'od_s5_c_re': nrm((NO, S5_GROUPS, S5_GROUP, S5_STATE), S5_STATE ** -0.5),
        'od_s5_c_im': nrm((NO, S5_GROUPS, S5_GROUP, S5_STATE), S5_STATE ** -0.5),
        'od_s5_d': nrm((NO, S5_WIDTH), 1.0),
        'od_glu_w': nrm((NO, S5_WIDTH, S5_WIDTH), S5_WIDTH ** -0.5),
        'od_glu_b': nrm((NO, S5_WIDTH), 0.02),
        'od_rw_mu': unif((NO, RW_IN), 0.0, 1.0),
        'od_rw_w0': unif((NO, RW_WIDTH), -4.0, 0.0),
        'od_rw_w2': nrm((NO, RW_DECAY_LORA, RW_WIDTH), 0.5 * RW_DECAY_LORA ** -0.5),
        'od_rw_a0': nrm((NO, RW_WIDTH), 0.1),
        'od_rw_a2': nrm((NO, RW_AAA_LORA, RW_WIDTH), 0.5 * RW_AAA_LORA ** -0.5),
        'od_rw_g2': nrm((NO, RW_GATE_LORA, RW_WIDTH), RW_GATE_LORA ** -0.5),
        'od_rw_kk': 0.85 + nrm((NO, RW_WIDTH), 0.02),
        'od_rw_ka': 1.0 + nrm((NO, RW_WIDTH), 0.02),
        'od_rw_rk': nrm((NO, RW_WIDTH), 0.1),
        'od_rw_ln_w': 1.0 + nrm((NO, RW_WIDTH), 0.02),
        'od_rw_ln_b': nrm((NO, RW_WIDTH), 0.02),
        'od_w_out': nrm((NO, MIX_WIDTH, D_MODEL), MIX_WIDTH ** -0.5),
        'ffn_w_gate': nrm((DEPTH, D_MODEL, FFN_HIDDEN), D_MODEL ** -0.5),
        'ffn_w_up': nrm((DEPTH, D_MODEL, FFN_HIDDEN), D_MODEL ** -0.5),
        'ffn_w_down': nrm((DEPTH, FFN_HIDDEN, D_MODEL), FFN_HIDDEN ** -0.5),
    }


def reference(x, meta_tokens, norm_mix_g, norm_ffn_g, final_norm_g,
              ev_w_in, ev_conv_w, ev_conv_b, ev_lru_wa, ev_lru_ba, ev_lru_wx, ev_lru_bx, ev_lru_lambda,
              ev_lq1, ev_lk1, ev_lq2, ev_lk2, ev_subln_g, ev_w_out,
              od_w_in, od_s5_lam_re, od_s5_lam_im, od_s5_log_dt, od_s5_b_re, od_s5_b_im,
              od_s5_c_re, od_s5_c_im, od_s5_d, od_glu_w, od_glu_b,
              od_rw_mu, od_rw_w0, od_rw_w2, od_rw_a0, od_rw_a2, od_rw_g2,
              od_rw_kk, od_rw_ka, od_rw_rk, od_rw_ln_w, od_rw_ln_b, od_w_out,
              ffn_w_gate, ffn_w_up, ffn_w_down):
    Bb = x.shape[0]
    meta = jnp.broadcast_to(meta_tokens[None].astype(x.dtype), (Bb, N_META, D_MODEL))
    h = jnp.concatenate([meta, x], axis=1)
    for layer in range(DEPTH):
        j = layer // 2
        hn = rmsnorm(h, norm_mix_g[layer])
        if layer % 2 == 0:
            lambda_init = 0.8 - 0.6 * math.exp(-0.3 * layer)
            z = hn @ ev_w_in[j]
            xa, ga, q, k, v = split_cols(z, (LRU_WIDTH, LRU_WIDTH, DA_WIDTH, DA_WIDTH, DA_WIDTH))
            xa = causal_depthwise_conv(xa, ev_conv_w[j], ev_conv_b[j])
            ya = rg_lru(xa, ev_lru_wa[j], ev_lru_ba[j], ev_lru_wx[j], ev_lru_bx[j], ev_lru_lambda[j])
            ya = ya * jax.nn.gelu(ga.astype(jnp.float32))
            yb = diff_attention(q, k, v, ev_lq1[j], ev_lk1[j], ev_lq2[j], ev_lk2[j], ev_subln_g[j], lambda_init)
            mix = jnp.concatenate([ya, yb], axis=-1) @ ev_w_out[j]
        else:
            z = hn @ od_w_in[j]
            u, dcols = z[..., :S5_WIDTH], z[..., S5_WIDTH:]
            yc = s5_layer(u, od_s5_lam_re[j], od_s5_lam_im[j], od_s5_log_dt[j], od_s5_b_re[j], od_s5_b_im[j],
                          od_s5_c_re[j], od_s5_c_im[j], od_s5_d[j])
            yc = jax.nn.gelu(yc)
            yc = yc * jax.nn.sigmoid(yc @ od_glu_w[j].astype(jnp.float32) + od_glu_b[j].astype(jnp.float32))
            yd = rwkv7_time_mix(dcols, od_rw_mu[j], od_rw_w0[j], od_rw_w2[j], od_rw_a0[j], od_rw_a2[j],
                                od_rw_g2[j], od_rw_kk[j], od_rw_ka[j], od_rw_rk[j], od_rw_ln_w[j], od_rw_ln_b[j])
            mix = jnp.concatenate([yc, yd], axis=-1) @ od_w_out[j]
        h = h + mix.astype(h.dtype)
        h = h + swiglu(rmsnorm(h, norm_ffn_g[layer]), ffn_w_gate[layer], ffn_w_up[layer], ffn_w_down[layer]).astype(h.dtype)
    return rmsnorm(h, final_norm_g)[:, N_META:]
```

```python
import functools
import math

import jax
import jax.numpy as jnp
from jax import lax
from jax.experimental import pallas as pl
from jax.experimental.pallas import tpu as pltpu

F32 = jnp.float32
BF16 = jnp.bfloat16

D_MODEL = 2048
N_META = 16
NORM_EPS = 1e-6
LRU_WIDTH = 1024
LRU_BLOCKS = 8
LRU_BLOCK = 128
CONV_WIDTH = 4
LRU_C = 8.0
DA_WIDTH = 1024
DA_HEADS = 8
DA_HEAD_DIM = 128
DA_QK_DIM = 64
S5_WIDTH = 1024
S5_GROUP = 16
S5_GROUPS = 64
S5_STATE = 64
RW_WIDTH = 1024
RW_HEAD_DIM = 64
RW_HEADS = 16
RW_GN_EPS = 64e-5
RW_DECAY_LORA = 96
RW_AAA_LORA = 96
RW_GATE_LORA = 256
RW_LORA_PAD = 128
FFN_HIDDEN = 5632

SUBLANES = 8
LANES = 128
VMEM_LIMIT_BYTES = 56 * 1024 * 1024

ROW_TILE = 640
RW_CHUNK = 64
RW_GROUP = 4
NEG_BIG = -1e30


def _cparams(sem):
    return pltpu.CompilerParams(dimension_semantics=sem, vmem_limit_bytes=VMEM_LIMIT_BYTES)


def _dot(a, b):
    return jnp.dot(a, b, preferred_element_type=F32)


def _dot_nt(a, b):
    return lax.dot_general(a, b, (((1,), (1,)), ((), ())), preferred_element_type=F32)


def _dot_tn(a, b):
    return lax.dot_general(a, b, (((0,), (0,)), ((), ())), preferred_element_type=F32)


def _split_dot(x, w_bf16):
    hi = x.astype(BF16)
    lo = (x - hi.astype(F32)).astype(BF16)
    return _dot(hi, w_bf16) + _dot(lo, w_bf16)


def _softplus(x):
    return jnp.maximum(x, 0.0) + jnp.log1p(jnp.exp(-jnp.abs(x)))


def _gelu_tanh(x):
    return x * (0.5 * (1.0 + jnp.tanh(0.7978845608028654 * (x + 0.044715 * (x * x * x)))))


def _rms_rows(x, g, eps):
    ms = jnp.mean(x * x, axis=-1, keepdims=True)
    return x * lax.rsqrt(ms + eps) * g


def _rms_matmul_kernel(h_ref, g_ref, w_ref, o_ref, hn_ref):
    @pl.when(pl.program_id(1) == 0)
    def _():
        hn_ref[...] = _rms_rows(h_ref[...], g_ref[...], NORM_EPS).astype(BF16)

    o_ref[...] = _dot(hn_ref[...], w_ref[...])


def _rms_matmul(h, g, w, tn=512):
    n, d = h.shape
    n_out = w.shape[1]
    tm = ROW_TILE
    return pl.pallas_call(
        _rms_matmul_kernel,
        grid=(n // tm, n_out // tn),
        in_specs=[
            pl.BlockSpec((tm, d), lambda i, j: (i, 0)),
            pl.BlockSpec((1, d), lambda i, j: (0, 0)),
            pl.BlockSpec((d, tn), lambda i, j: (0, j)),
        ],
        out_specs=pl.BlockSpec((tm, tn), lambda i, j: (i, j)),
        out_shape=jax.ShapeDtypeStruct((n, n_out), F32),
        scratch_shapes=[pltpu.VMEM((tm, d), BF16)],
        compiler_params=_cparams(("parallel", "arbitrary")),
        name="rms_matmul",
    )(h, g.reshape(1, d), w)


def _out_proj_kernel(h_ref, ya_ref, yb_ref, wa_ref, wb_ref, o_ref):
    o_ref[...] = h_ref[...] + _dot(ya_ref[...], wa_ref[...]) + _dot(yb_ref[...], wb_ref[...])


def _out_proj(h, ya, yb, wa, wb, tn=1024):
    n, d = h.shape
    ka, kb = ya.shape[1], yb.shape[1]
    tm = ROW_TILE
    return pl.pallas_call(
        _out_proj_kernel,
        grid=(n // tm, d // tn),
        in_specs=[
            pl.BlockSpec((tm, tn), lambda i, j: (i, j)),
            pl.BlockSpec((tm, ka), lambda i, j: (i, 0)),
            pl.BlockSpec((tm, kb), lambda i, j: (i, 0)),
            pl.BlockSpec((ka, tn), lambda i, j: (0, j)),
            pl.BlockSpec((kb, tn), lambda i, j: (0, j)),
        ],
        out_specs=pl.BlockSpec((tm, tn), lambda i, j: (i, j)),
        out_shape=jax.ShapeDtypeStruct((n, d), F32),
        compiler_params=_cparams(("parallel", "arbitrary")),
        name="out_proj",
    )(h, ya, yb, wa, wb)


def _ffn_kernel(h_ref, g_ref, wg_ref, wu_ref, wd_ref, o_ref, hn_ref, acc_ref):
    j = pl.program_id(1)

    @pl.when(j == 0)
    def _():
        hn_ref[...] = _rms_rows(h_ref[...], g_ref[...], NORM_EPS).astype(BF16)
        acc_ref[...] = jnp.zeros_like(acc_ref)

    hn = hn_ref[...]
    gate = _dot(hn, wg_ref[...])
    up = _dot(hn, wu_ref[...])
    act = (gate * jax.nn.sigmoid(gate) * up).astype(BF16)
    acc_ref[...] += _dot(act, wd_ref[...])

    @pl.when(j == pl.num_programs(1) - 1)
    def _():
        o_ref[...] = h_ref[...] + acc_ref[...]


def _ffn(h, g, wg, wu, wd, th=512):
    n, d = h.shape
    hid = wg.shape[1]
    tm = ROW_TILE
    return pl.pallas_call(
        _ffn_kernel,
        grid=(n // tm, hid // th),
        in_specs=[
            pl.BlockSpec((tm, d), lambda i, j: (i, 0)),
            pl.BlockSpec((1, d), lambda i, j: (0, 0)),
            pl.BlockSpec((d, th), lambda i, j: (0, j)),
            pl.BlockSpec((d, th), lambda i, j: (0, j)),
            pl.BlockSpec((th, d), lambda i, j: (j, 0)),
        ],
        out_specs=pl.BlockSpec((tm, d), lambda i, j: (i, 0)),
        out_shape=jax.ShapeDtypeStruct((n, d), F32),
        scratch_shapes=[pltpu.VMEM((tm, d), BF16), pltpu.VMEM((tm, d), F32)],
        compiler_params=_cparams(("parallel", "arbitrary")),
        name="ffn",
    )(h, g.reshape(1, d), wg, wu, wd)


def _final_norm_kernel(h_ref, g_ref, o_ref):
    o_ref[...] = _rms_rows(h_ref[...], g_ref[...], NORM_EPS)


def _final_norm(h, g):
    n, d = h.shape
    tm = ROW_TILE
    return pl.pallas_call(
        _final_norm_kernel,
        grid=(n // tm,),
        in_specs=[pl.BlockSpec((tm, d), lambda i: (i, 0)), pl.BlockSpec((1, d), lambda i: (0, 0))],
        out_specs=pl.BlockSpec((tm, d), lambda i: (i, 0)),
        out_shape=jax.ShapeDtypeStruct((n, d), F32),
        compiler_params=_cparams(("parallel",)),
        name="final_norm",
    )(h, g.reshape(1, d))


def _lru_kernel(xa_ref, ga_ref, cw_ref, cb_ref, wg_ref, ba_ref, bx_ref, lam_ref, o_ref,
                xe_ref, a_ref, b_ref, hc_ref):
    tt = xa_ref.shape[0]
    t = pl.program_id(1)

    @pl.when(t == 0)
    def _():
        xe_ref[0:SUBLANES, :] = jnp.zeros((SUBLANES, LRU_WIDTH), F32)
        hc_ref[...] = jnp.zeros_like(hc_ref)

    xe_ref[SUBLANES:SUBLANES + tt, :] = xa_ref[...]
    u = cb_ref[...]
    for j in range(CONV_WIDTH):
        off = SUBLANES - (CONV_WIDTH - 1) + j
        u = u + xe_ref[off:off + tt, :] * cw_ref[j:j + 1, :]
    xe_ref[0:SUBLANES, :] = xa_ref[tt - SUBLANES:tt, :]

    sp = _softplus(-lam_ref[...])
    for n in range(LRU_BLOCKS):
        sl = slice(n * LRU_BLOCK, (n + 1) * LRU_BLOCK)
        un = u[:, sl]
        zz = _dot(un.astype(BF16), wg_ref[n])
        r = jax.nn.sigmoid(zz[:, :LRU_BLOCK] + ba_ref[:, sl])
        i = jax.nn.sigmoid(zz[:, LRU_BLOCK:] + bx_ref[:, sl])
        log_a = (-LRU_C) * r * sp[:, sl]
        a = jnp.exp(log_a)
        gain = jnp.sqrt(-jnp.tanh(log_a) * (a * a + 1.0))
        a_ref[:, sl] = a
        b_ref[:, sl] = gain * (i * un)

    row = lax.broadcasted_iota(jnp.int32, (SUBLANES, LRU_WIDTH), 0)

    def body(gi, carry):
        r0 = pl.multiple_of(gi * SUBLANES, SUBLANES)
        a = a_ref[pl.ds(r0, SUBLANES), :]
        b = b_ref[pl.ds(r0, SUBLANES), :]
        for s in (1, 2, 4):
            keep = row >= s
            a_sh = jnp.where(keep, pltpu.roll(a, s, 0), 1.0)
            b_sh = jnp.where(keep, pltpu.roll(b, s, 0), 0.0)
            b = b + a * b_sh
            a = a * a_sh
        hblk = a * carry + b
        b_ref[pl.ds(r0, SUBLANES), :] = hblk
        return hblk[SUBLANES - 1:SUBLANES, :]

    hc_ref[...] = lax.fori_loop(0, tt // SUBLANES, body, hc_ref[...])
    o_ref[...] = (b_ref[...] * _gelu_tanh(ga_ref[...])).astype(BF16)


def _lru_mixer(z, nb, cw, cb, wg, ba, bx, lam, tt=ROW_TILE):
    n = z.shape[0]
    nt = n // nb // tt
    w = LRU_WIDTH
    row = lambda c: pl.BlockSpec((1, w), lambda b, t: (0, 0))
    return pl.pallas_call(
        _lru_kernel,
        grid=(nb, nt),
        in_specs=[
            pl.BlockSpec((tt, w), lambda b, t: (b * nt + t, 0)),
            pl.BlockSpec((tt, w), lambda b, t: (b * nt + t, 1)),
            pl.BlockSpec((CONV_WIDTH, w), lambda b, t: (0, 0)),
            row(0),
            pl.BlockSpec((LRU_BLOCKS, LRU_BLOCK, 2 * LRU_BLOCK), lambda b, t: (0, 0, 0)),
            row(0), row(0), row(0),
        ],
        out_specs=pl.BlockSpec((tt, w), lambda b, t: (b * nt + t, 0)),
        out_shape=jax.ShapeDtypeStruct((n, w), BF16),
        scratch_shapes=[
            pltpu.VMEM((tt + SUBLANES, w), F32),
            pltpu.VMEM((tt, w), F32),
            pltpu.VMEM((tt, w), F32),
            pltpu.VMEM((1, w), F32),
        ],
        compiler_params=_cparams(("parallel", "arbitrary")),
        name="lru_mixer",
    )(z, z, cw, cb.reshape(1, w), wg, ba.reshape(1, w), bx.reshape(1, w), lam.reshape(1, w))


def _attn_kernel(qi_tab, ki_tab, slopes_ref, q_ref, k_ref, v_ref, lq1_ref, lk1_ref, lq2_ref, lk2_ref,
                 g_ref, o_ref, m_ref, l_ref, acc_ref, *, lambda_init):
    tq = q_ref.shape[0]
    tk = k_ref.shape[0]
    h = pl.program_id(1)
    p = pl.program_id(2)
    qi = qi_tab[p]
    ki = ki_tab[p]

    @pl.when(ki == 0)
    def _():
        m_ref[...] = jnp.full_like(m_ref, NEG_BIG)
        l_ref[...] = jnp.zeros_like(l_ref)
        acc_ref[...] = jnp.zeros_like(acc_ref)

    q = q_ref[...] * (DA_QK_DIM ** -0.5)
    kb = k_ref[...].astype(BF16)
    vb = v_ref[...].astype(BF16)
    lane = lax.broadcasted_iota(jnp.int32, (1, DA_HEAD_DIM), 1)
    qpos = qi * tq + lax.broadcasted_iota(jnp.int32, (tq, 1), 0)
    kpos = ki * tk + lax.broadcasted_iota(jnp.int32, (1, tk), 1)
    bias = slopes_ref[h] * (kpos - qpos).astype(F32)
    visible = kpos <= qpos
    for m in range(2):
        sel = (lane < DA_QK_DIM) if m == 0 else (lane >= DA_QK_DIM)
        qm = jnp.where(sel, q, 0.0).astype(BF16)
        s = _dot_nt(qm, kb) + bias
        s = jnp.where(visible, s, NEG_BIG)
        m_prev = m_ref[m]
        m_new = jnp.maximum(m_prev, jnp.max(s, axis=-1, keepdims=True))
        alpha = jnp.exp(m_prev - m_new)
        pm = jnp.exp(s - m_new)
        l_ref[m] = alpha * l_ref[m] + jnp.sum(pm, axis=-1, keepdims=True)
        acc_ref[m] = alpha * acc_ref[m] + _dot(pm.astype(BF16), vb)
        m_ref[m] = m_new

    @pl.when(ki == qi)
    def _():
        lam = (jnp.exp(jnp.sum(lq1_ref[...] * lk1_ref[...], axis=-1, keepdims=True))
               - jnp.exp(jnp.sum(lq2_ref[...] * lk2_ref[...], axis=-1, keepdims=True)) + lambda_init)
        o = acc_ref[0] / l_ref[0] - lam * (acc_ref[1] / l_ref[1])
        o = _rms_rows(o, g_ref[...], 1e-5) * (1.0 - lambda_init)
        o_ref[...] = o.astype(BF16)


def _diff_attention(z, nb, lq1, lk1, lq2, lk2, subln_g, lambda_init, col0, tq=ROW_TILE):
    n = z.shape[0]
    nq = n // nb // tq
    pairs = [(qi, ki) for qi in range(nq) for ki in range(qi + 1)]
    qi_tab = jnp.asarray([p[0] for p in pairs], jnp.int32)
    ki_tab = jnp.asarray([p[1] for p in pairs], jnp.int32)
    slopes = 2.0 ** (-8.0 * jnp.arange(1, DA_HEADS + 1, dtype=F32) / DA_HEADS)
    e = DA_HEAD_DIM
    cq, ck, cv = col0 // e, (col0 + DA_WIDTH) // e, (col0 + 2 * DA_WIDTH) // e
    vec = lambda: pl.BlockSpec((1, DA_QK_DIM), lambda b, h, p, qt, kt: (0, 0))
    grid_spec = pltpu.PrefetchScalarGridSpec(
        num_scalar_prefetch=2,
        grid=(nb, DA_HEADS, len(pairs)),
        in_specs=[
            pl.BlockSpec(memory_space=pltpu.SMEM),
            pl.BlockSpec((tq, e), lambda b, h, p, qt, kt: (b * nq + qt[p], cq + h)),
            pl.BlockSpec((tq, e), lambda b, h, p, qt, kt: (b * nq + kt[p], ck + h)),
            pl.BlockSpec((tq, e), lambda b, h, p, qt, kt: (b * nq + kt[p], cv + h)),
            vec(), vec(), vec(), vec(),
            pl.BlockSpec((1, e), lambda b, h, p, qt, kt: (0, 0)),
        ],
        out_specs=pl.BlockSpec((tq, e), lambda b, h, p, qt, kt: (b * nq + qt[p], h)),
        scratch_shapes=[
            pltpu.VMEM((2, tq, 1), F32),
            pltpu.VMEM((2, tq, 1), F32),
            pltpu.VMEM((2, tq, e), F32),
        ],
    )
    r64 = lambda a: a.reshape(1, DA_QK_DIM)
    return pl.pallas_call(
        functools.partial(_attn_kernel, lambda_init=lambda_init),
        grid_spec=grid_spec,
        out_shape=jax.ShapeDtypeStruct((n, DA_WIDTH), BF16),
        compiler_params=_cparams(("parallel", "parallel", "arbitrary")),
        name="diff_attention",
    )(qi_tab, ki_tab, slopes, z, z, z, r64(lq1), r64(lk1), r64(lq2), r64(lk2), subln_g.reshape(1, e))


def _s5_disc_kernel(lr_ref, li_ref, ldt_ref, ar_ref, ai_ref, cr_ref, ci_ref):
    lr, li = lr_ref[...], li_ref[...]
    dt = jnp.exp(ldt_ref[...])
    mag = jnp.exp(lr * dt)
    ar = mag * jnp.cos(li * dt)
    ai = mag * jnp.sin(li * dt)
    den = lr * lr + li * li
    ar_ref[...] = ar
    ai_ref[...] = ai
    cr_ref[...] = ((ar - 1.0) * lr + ai * li) / den
    ci_ref[...] = (ai * lr - (ar - 1.0) * li) / den


def _s5_discretise(lam_re, lam_im, log_dt):
    shp = jax.ShapeDtypeStruct(lam_re.shape, F32)
    ldt = jnp.broadcast_to(log_dt[:, None], lam_re.shape)
    return pl.pallas_call(_s5_disc_kernel, out_shape=(shp, shp, shp, shp), name="s5_discretise")(
        lam_re, lam_im, ldt)


S5_KB = 4
S5_KB_CH = S5_WIDTH // S5_KB
S5_KB_ST = S5_GROUPS // S5_KB * S5_STATE


def _s5_kernel(u_ref, wb_ref, wc_ref, ar_ref, ai_ref, d_ref, gw_ref, gb_ref, o_ref,
               x_ref, pr_ref, pi_ref, cr_ref, ci_ref, y_ref):
    tt = u_ref.shape[0]
    t = pl.program_id(1)
    ns = S5_KB_ST

    @pl.when(t == 0)
    def _():
        cr_ref[...] = jnp.zeros_like(cr_ref)
        ci_ref[...] = jnp.zeros_like(ci_ref)
        for kb in range(S5_KB):
            ar, ai = ar_ref[kb:kb + 1, :], ai_ref[kb:kb + 1, :]
            pr, pi = ar, ai
            for j in range(SUBLANES):
                pr_ref[kb, j:j + 1, :] = pr
                pi_ref[kb, j:j + 1, :] = pi
                pr, pi = pr * ar - pi * ai, pr * ai + pi * ar

    u = u_ref[...]
    for kb in range(S5_KB):
        ukb = u[:, kb * S5_KB_CH:(kb + 1) * S5_KB_CH].astype(BF16)
        x_ref[kb] = _dot(ukb, wb_ref[kb])

    row = lax.broadcasted_iota(jnp.int32, (SUBLANES, ns), 0)

    def body(gi, carry):
        r0 = pl.multiple_of(gi * SUBLANES, SUBLANES)
        new = []
        for kb in range(S5_KB):
            cr, ci = carry[2 * kb], carry[2 * kb + 1]
            xr = x_ref[kb, pl.ds(r0, SUBLANES), 0:ns]
            xi = x_ref[kb, pl.ds(r0, SUBLANES), ns:2 * ns]
            for s in (1, 2, 4):
                keep = row >= s
                ar = pr_ref[kb, s - 1:s, :]
                ai = pi_ref[kb, s - 1:s, :]
                sr = jnp.where(keep, pltpu.roll(xr, s, 0), 0.0)
                si = jnp.where(keep, pltpu.roll(xi, s, 0), 0.0)
                xr, xi = xr + ar * sr - ai * si, xi + ar * si + ai * sr
            pr, pi = pr_ref[kb], pi_ref[kb]
            xr, xi = xr + pr * cr - pi * ci, xi + pr * ci + pi * cr
            x_ref[kb, pl.ds(r0, SUBLANES), 0:ns] = xr
            x_ref[kb, pl.ds(r0, SUBLANES), ns:2 * ns] = xi
            new += [xr[SUBLANES - 1:SUBLANES, :], xi[SUBLANES - 1:SUBLANES, :]]
        return tuple(new)

    init = []
    for kb in range(S5_KB):
        init += [cr_ref[kb:kb + 1, :], ci_ref[kb:kb + 1, :]]
    fin = lax.fori_loop(0, tt // SUBLANES, body, tuple(init))
    for kb in range(S5_KB):
        cr_ref[kb:kb + 1, :] = fin[2 * kb]
        ci_ref[kb:kb + 1, :] = fin[2 * kb + 1]

    for kb in range(S5_KB):
        y_ref[:, kb * S5_KB_CH:(kb + 1) * S5_KB_CH] = _dot(x_ref[kb].astype(BF16), wc_ref[kb])
    y = _gelu_tanh(y_ref[...] + d_ref[...] * u)
    gate = jax.nn.sigmoid(_dot(y.astype(BF16), gw_ref[...]) + gb_ref[...])
    o_ref[...] = (y * gate).astype(BF16)


def _s5_mixer(z, nb, wb, wc, ar, ai, d_skip, glu_w, glu_b, tt=128):
    n = z.shape[0]
    nt = n // nb // tt
    w = S5_WIDTH
    full = lambda shape: pl.BlockSpec(shape, lambda b, t: (0,) * len(shape))
    return pl.pallas_call(
        _s5_kernel,
        grid=(nb, nt),
        in_specs=[
            pl.BlockSpec((tt, w), lambda b, t: (b * nt + t, 0)),
            full((S5_KB, S5_KB_CH, 2 * S5_KB_ST)),
            full((S5_KB, 2 * S5_KB_ST, S5_KB_CH)),
            full((S5_KB, S5_KB_ST)),
            full((S5_KB, S5_KB_ST)),
            full((1, w)),
            full((w, w)),
            full((1, w)),
        ],
        out_specs=pl.BlockSpec((tt, w), lambda b, t: (b * nt + t, 0)),
        out_shape=jax.ShapeDtypeStruct((n, w), BF16),
        scratch_shapes=[
            pltpu.VMEM((S5_KB, tt, 2 * S5_KB_ST), F32),
            pltpu.VMEM((S5_KB, SUBLANES, S5_KB_ST), F32),
            pltpu.VMEM((S5_KB, SUBLANES, S5_KB_ST), F32),
            pltpu.VMEM((S5_KB, S5_KB_ST), F32),
            pltpu.VMEM((S5_KB, S5_KB_ST), F32),
            pltpu.VMEM((tt, w), F32),
        ],
        compiler_params=_cparams(("parallel", "arbitrary")),
        name="s5_mixer",
    )(z, wb, wc, ar, ai, d_skip.reshape(1, w), glu_w, glu_b.reshape(1, w))


def _token_shift(x_ref, mu_ref, xe_ref, first):
    tt = x_ref.shape[0]

    @pl.when(first)
    def _():
        xe_ref[0:SUBLANES, :] = jnp.zeros((SUBLANES, x_ref.shape[1]), F32)

    x = x_ref[...]
    xe_ref[SUBLANES:SUBLANES + tt, :] = x
    xprev = xe_ref[SUBLANES - 1:SUBLANES - 1 + tt, :]
    xe_ref[0:SUBLANES, :] = x_ref[tt - SUBLANES:tt, :]
    return x + (xprev - x) * mu_ref[...]


def _rw_prep_kernel(xr_ref, xk_ref, xv_ref, xl_ref, mur_ref, muk_ref, muv_ref, mul_ref,
                    w0_ref, w2_ref, a0_ref, a2_ref, g2_ref, kk_ref, ka_ref, ones_ref,
                    r_out, lw_out, k_out, v_out, kk_out, a_out, g_out,
                    er_ref, ek_ref, ev_ref, el_ref):
    first = pl.program_id(1) == 0
    r = _token_shift(xr_ref, mur_ref, er_ref, first)
    k = _token_shift(xk_ref, muk_ref, ek_ref, first)
    v = _token_shift(xv_ref, muv_ref, ev_ref, first)
    lo = _token_shift(xl_ref, mul_ref, el_ref, first)
    wl = lo[:, 0:RW_LORA_PAD]
    al = lo[:, RW_LORA_PAD:2 * RW_LORA_PAD]
    gl = lo[:, 2 * RW_LORA_PAD:2 * RW_LORA_PAD + RW_GATE_LORA]

    wdec = -_softplus(-(w0_ref[...] + _dot(jnp.tanh(wl).astype(BF16), w2_ref[...]))) - 0.5
    a = jax.nn.sigmoid(a0_ref[...] + _dot(al.astype(BF16), a2_ref[...]))
    g = _dot(jax.nn.sigmoid(gl).astype(BF16), g2_ref[...])
    kkr = k * kk_ref[...]
    ssq = _split_dot(kkr * kkr, ones_ref[...])
    kk = kkr / jnp.maximum(jnp.sqrt(ssq), 1e-12)

    r_out[...] = r
    lw_out[...] = -jnp.exp(wdec)
    k_out[...] = k * (1.0 + (a - 1.0) * ka_ref[...])
    v_out[...] = v
    kk_out[...] = kk
    a_out[...] = a
    g_out[...] = g


RW_LORA_COLS = 2 * RW_LORA_PAD + RW_GATE_LORA


def _rw_prep(z, nb, col0, mu, w0, w2, a0, a2, g2, k_k, k_a, ones_bd, tt=320):
    n = z.shape[0]
    nt = n // nb // tt
    w = RW_WIDTH
    lw = RW_LORA_COLS
    assert col0 % w == 0 and (col0 + 3 * w) % lw == 0
    cb = col0 // w
    cl = (col0 + 3 * w) // lw
    full = lambda shape: pl.BlockSpec(shape, lambda b, t: (0,) * len(shape))
    xblk = lambda width, c: pl.BlockSpec((tt, width), lambda b, t: (b * nt + t, c))
    out = jax.ShapeDtypeStruct((n, w), F32)
    ospec = pl.BlockSpec((tt, w), lambda b, t: (b * nt + t, 0))
    mu = mu.reshape(1, 3 * w + lw)
    return pl.pallas_call(
        _rw_prep_kernel,
        grid=(nb, nt),
        in_specs=[
            xblk(w, cb), xblk(w, cb + 1), xblk(w, cb + 2), xblk(lw, cl),
            full((1, w)), full((1, w)), full((1, w)), full((1, lw)),
            full((1, w)), full((RW_LORA_PAD, w)), full((1, w)), full((RW_LORA_PAD, w)),
            full((RW_GATE_LORA, w)), full((1, w)), full((1, w)), full((w, w)),
        ],
        out_specs=[ospec] * 7,
        out_shape=[out] * 7,
        scratch_shapes=[pltpu.VMEM((tt + SUBLANES, w), F32)] * 3 + [pltpu.VMEM((tt + SUBLANES, lw), F32)],
        compiler_params=_cparams(("parallel", "arbitrary")),
        name="rwkv_prep",
    )(z, z, z, z, mu[:, 0:w], mu[:, w:2 * w], mu[:, 2 * w:3 * w], mu[:, 3 * w:],
      w0.reshape(1, w), w2, a0.reshape(1, w), a2, g2, k_k.reshape(1, w), k_a.reshape(1, w), ones_bd)


def _rw_chunk_kernel(r_ref, lw_ref, k_ref, v_ref, kk_ref, a_ref, g_ref, rk_ref, lnw_ref, lnb_ref,
                     ones_ref, o_ref, s_ref):
    L = RW_CHUNK
    gw = RW_GROUP * RW_HEAD_DIM
    gl_rows = RW_GROUP * L
    c = pl.program_id(2)

    @pl.when(c == 0)
    def _():
        s_ref[...] = jnp.zeros_like(s_ref)

    r, lw, k, v, kk, a = r_ref[...], lw_ref[...], k_ref[...], v_ref[...], kk_ref[...], a_ref[...]

    ti = lax.broadcasted_iota(jnp.int32, (L, L), 0)
    tj = lax.broadcasted_iota(jnp.int32, (L, L), 1)
    tri = jnp.where(ti >= tj, 1.0, 0.0).astype(BF16)
    x1 = lw.astype(BF16)
    r1 = lw - x1.astype(F32)
    x2 = r1.astype(BF16)
    x3 = (r1 - x2.astype(F32)).astype(BF16)
    cl = _dot(tri, x1) + _dot(tri, x2) + _dot(tri, x3)
    g_last = jnp.exp(cl[L - 1:L, :])
    e_in = jnp.exp(cl)
    e_ex = jnp.exp(cl - lw)
    e_inv = jnp.exp(-cl)
    beta = kk * a
    a_t = -kk * e_ex
    r_t = r * e_in
    b_t = beta * e_inv
    k_t = k * e_inv

    lane_head = lax.shift_right_logical(lax.broadcasted_iota(jnp.int32, (L, gw), 1),
                                        int(math.log2(RW_HEAD_DIM)))

    def stack(x):
        return jnp.concatenate([jnp.where(lane_head == hh, x, 0.0) for hh in range(RW_GROUP)],
                               axis=0).astype(BF16)

    a_s, r_s, b_s, k_s, v_s = stack(a_t), stack(r_t), stack(b_t), stack(k_t), stack(v)
    bh_s, kh_s = stack(b_t * g_last), stack(k_t * g_last)

    ar_s = jnp.concatenate([a_s, r_s], axis=0)
    bk_s = jnp.concatenate([b_s, k_s], axis=0)
    prod = _dot_nt(ar_s, bk_s)
    ri = lax.broadcasted_iota(jnp.int32, (gl_rows, gl_rows), 0)
    ci = lax.broadcasted_iota(jnp.int32, (gl_rows, gl_rows), 1)
    strict = ri > ci
    incl = ri >= ci
    n_ab = jnp.where(strict, prod[:gl_rows, :gl_rows], 0.0)
    a_ak = jnp.where(strict, prod[:gl_rows, gl_rows:], 0.0).astype(BF16)
    m_rb = jnp.where(incl, prod[gl_rows:, :gl_rows], 0.0).astype(BF16)
    m_rk = jnp.where(incl, prod[gl_rows:, gl_rows:], 0.0).astype(BF16)

    s0 = s_ref[...]
    w0 = _dot_nt(ar_s, s0.astype(BF16))
    x = w0[:gl_rows] + _dot(a_ak, v_s)
    npow = n_ab.astype(BF16)
    steps = int(math.log2(L))
    for kx in range(steps):
        x = x + _dot(npow, x.astype(BF16))
        if kx + 1 < steps:
            npow = _dot(npow, npow).astype(BF16)
    u_s = x.astype(BF16)

    y_s = w0[gl_rows:] + _dot(m_rb, u_s) + _dot(m_rk, v_s)
    y = y_s[0:L]
    for hh in range(1, RW_GROUP):
        y = y + y_s[hh * L:(hh + 1) * L]

    s_ref[...] = s0 * g_last + _dot_tn(u_s, bh_s) + _dot_tn(v_s, kh_s)

    ones = ones_ref[...]
    inv_n = 1.0 / RW_HEAD_DIM
    mean = _split_dot(y, ones) * inv_n
    yc = y - mean
    var = _split_dot(yc * yc, ones) * inv_n
    yn = yc * lax.rsqrt(var + RW_GN_EPS) * lnw_ref[...] + lnb_ref[...]
    bonus = _split_dot(r * k * rk_ref[...], ones) * v
    o_ref[...] = ((yn + bonus) * g_ref[...]).astype(BF16)


def _rw_chunk(prep, nb, r_k, ln_w, ln_b, ones_g):
    r, lw, k, v, kk, a, g = prep
    n = r.shape[0]
    L = RW_CHUNK
    gw = RW_GROUP * RW_HEAD_DIM
    ng = RW_WIDTH // gw
    nc = n // nb // L
    blk = lambda: pl.BlockSpec((L, gw), lambda b, gi, c: (b * nc + c, gi))
    par = lambda: pl.BlockSpec((1, gw), lambda b, gi, c: (0, gi))
    w = RW_WIDTH
    return pl.pallas_call(
        _rw_chunk_kernel,
        grid=(nb, ng, nc),
        in_specs=[blk() for _ in range(7)] + [par(), par(), par(),
                                             pl.BlockSpec((gw, gw), lambda b, gi, c: (0, 0))],
        out_specs=blk(),
        out_shape=jax.ShapeDtypeStruct((n, w), BF16),
        scratch_shapes=[pltpu.VMEM((gw, gw), F32)],
        compiler_params=_cparams(("parallel", "parallel", "arbitrary")),
        name="rwkv_chunk",
    )(r, lw, k, v, kk, a, g, r_k.reshape(1, w), ln_w.reshape(1, w), ln_b.reshape(1, w), ones_g)


def _head_ones(width, head):
    idx = jnp.arange(width) // head
    return (idx[:, None] == idx[None, :]).astype(BF16)


def _pack_lru_gates(wa, wx):
    return jnp.concatenate([wa, wx], axis=-1).astype(BF16)


def _pack_odd_in(w_in, mu):
    s5 = S5_WIDTH
    c_rkv = s5 + 3 * RW_WIDTH
    c_wl = c_rkv + RW_DECAY_LORA
    c_al = c_wl + RW_AAA_LORA
    padw = lambda x, n: jnp.pad(x, ((0, 0), (0, n - x.shape[1])))
    w = jnp.concatenate([
        w_in[:, :c_rkv],
        padw(w_in[:, c_rkv:c_wl], RW_LORA_PAD),
        padw(w_in[:, c_wl:c_al], RW_LORA_PAD),
        w_in[:, c_al:],
    ], axis=1)
    m = mu[None, :]
    o = s5
    mu_p = jnp.concatenate([
        m[:, :c_rkv - o],
        padw(m[:, c_rkv - o:c_wl - o], RW_LORA_PAD),
        padw(m[:, c_wl - o:c_al - o], RW_LORA_PAD),
        m[:, c_al - o:],
    ], axis=1)[0]
    return w.astype(BF16), mu_p


def _pad_rows(x, n):
    return jnp.pad(x, ((0, n - x.shape[0]), (0, 0)))


def _pack_s5(cr, ci, b_re, b_im, c_re, c_im):
    gpb = S5_GROUPS // S5_KB
    bbr = cr[..., None] * b_re - ci[..., None] * b_im
    bbi = cr[..., None] * b_im + ci[..., None] * b_re
    eye = jnp.eye(gpb, dtype=F32)

    def in_proj(bb):
        x = bb.reshape(S5_KB, gpb, S5_STATE, S5_GROUP)
        return jnp.einsum("kgpc,gh->kgchp", x, eye).reshape(S5_KB, S5_KB_CH, S5_KB_ST)

    def out_proj(cc):
        x = cc.reshape(S5_KB, gpb, S5_GROUP, S5_STATE)
        return jnp.einsum("kgcp,gh->kgphc", x, eye).reshape(S5_KB, S5_KB_ST, S5_KB_CH)

    wb = jnp.concatenate([in_proj(bbr), in_proj(bbi)], axis=2).astype(BF16)
    wc = jnp.concatenate([out_proj(c_re), out_proj(-c_im)], axis=1).astype(BF16)
    return wb, wc


def kernel(x, meta_tokens, norm_mix_g, norm_ffn_g, final_norm_g, ev_w_in, ev_conv_w, ev_conv_b, ev_lru_wa, ev_lru_ba, ev_lru_wx, ev_lru_bx, ev_lru_lambda, ev_lq1, ev_lk1, ev_lq2, ev_lk2, ev_subln_g, ev_w_out, od_w_in, od_s5_lam_re, od_s5_lam_im, od_s5_log_dt, od_s5_b_re, od_s5_b_im, od_s5_c_re, od_s5_c_im, od_s5_d, od_glu_w, od_glu_b, od_rw_mu, od_rw_w0, od_rw_w2, od_rw_a0, od_rw_a2, od_rw_g2, od_rw_kk, od_rw_ka, od_rw_rk, od_rw_ln_w, od_rw_ln_b, od_w_out, ffn_w_gate, ffn_w_up, ffn_w_down):
    nb, seq, d = x.shape
    depth = norm_mix_g.shape[0]
    t_real = N_META + seq
    tp = -(-t_real // ROW_TILE) * ROW_TILE
    meta = jnp.broadcast_to(meta_tokens[None].astype(x.dtype), (nb, N_META, d))
    pad = jnp.zeros((nb, tp - t_real, d), x.dtype)
    h = jnp.concatenate([meta, x, pad], axis=1).reshape(nb * tp, d)

    ones_rw = _head_ones(RW_WIDTH, RW_HEAD_DIM)
    ones_grp = _head_ones(RW_GROUP * RW_HEAD_DIM, RW_HEAD_DIM)

    for layer in range(depth):
        j = layer // 2
        if layer % 2 == 0:
            lambda_init = 0.8 - 0.6 * math.exp(-0.3 * layer)
            z = _rms_matmul(h, norm_mix_g[layer], ev_w_in[j].astype(BF16))
            ya = _lru_mixer(z, nb, ev_conv_w[j], ev_conv_b[j], _pack_lru_gates(ev_lru_wa[j], ev_lru_wx[j]),
                            ev_lru_ba[j], ev_lru_bx[j], ev_lru_lambda[j])
            yb = _diff_attention(z, nb, ev_lq1[j], ev_lk1[j], ev_lq2[j], ev_lk2[j], ev_subln_g[j],
                                 lambda_init, col0=2 * LRU_WIDTH)
            w_out = ev_w_out[j].astype(BF16)
            h = _out_proj(h, ya, yb, w_out[:LRU_WIDTH], w_out[LRU_WIDTH:])
        else:
            w_in, mu_p = _pack_odd_in(od_w_in[j], od_rw_mu[j])
            z = _rms_matmul(h, norm_mix_g[layer], w_in)
            ar, ai, cr, ci = _s5_discretise(od_s5_lam_re[j], od_s5_lam_im[j], od_s5_log_dt[j])
            wb, wc = _pack_s5(cr, ci, od_s5_b_re[j], od_s5_b_im[j], od_s5_c_re[j], od_s5_c_im[j])
            yc = _s5_mixer(z, nb, wb, wc, ar.reshape(S5_KB, S5_KB_ST), ai.reshape(S5_KB, S5_KB_ST),
                           od_s5_d[j], od_glu_w[j].astype(BF16), od_glu_b[j])
            prep = _rw_prep(z, nb, S5_WIDTH, mu_p, od_rw_w0[j],
                            _pad_rows(od_rw_w2[j], RW_LORA_PAD).astype(BF16), od_rw_a0[j],
                            _pad_rows(od_rw_a2[j], RW_LORA_PAD).astype(BF16), od_rw_g2[j].astype(BF16),
                            od_rw_kk[j], od_rw_ka[j], ones_rw)
            yd = _rw_chunk(prep, nb, od_rw_rk[j], od_rw_ln_w[j], od_rw_ln_b[j], ones_grp)
            w_out = od_w_out[j].astype(BF16)
            h = _out_proj(h, yc, yd, w_out[:S5_WIDTH], w_out[S5_WIDTH:])
        h = _ffn(h, norm_ffn_g[layer], ffn_w_gate[layer].astype(BF16), ffn_w_up[layer].astype(BF16),
                 ffn_w_down[layer].astype(BF16))
    out = _final_norm(h, final_norm_g).reshape(nb, tp, d)
    return out[:, N_META:t_real]
```

```python
import functools
import math

import jax
import jax.numpy as jnp
from jax import lax
from jax.experimental import pallas as pl
from jax.experimental.pallas import tpu as pltpu

F32 = jnp.float32
BF16 = jnp.bfloat16

D_MODEL = 2048
N_META = 16
NORM_EPS = 1e-6
LRU_WIDTH = 1024
LRU_BLOCKS = 8
LRU_BLOCK = 128
CONV_WIDTH = 4
LRU_C = 8.0
DA_WIDTH = 1024
DA_HEADS = 8
DA_HEAD_DIM = 128
DA_QK_DIM = 64
S5_WIDTH = 1024
S5_GROUP = 16
S5_GROUPS = 64
S5_STATE = 64
RW_WIDTH = 1024
RW_HEAD_DIM = 64
RW_HEADS = 16
RW_GN_EPS = 64e-5
RW_DECAY_LORA = 96
RW_AAA_LORA = 96
RW_GATE_LORA = 256
RW_LORA_PAD = 128
FFN_HIDDEN = 5632

SUBLANES = 8
LANES = 128
VMEM_LIMIT_BYTES = 56 * 1024 * 1024

ROW_TILE = 640
RW_CHUNK = 64
RW_GROUP = 4
RW_STEP_GROUPS = 2
NEG_BIG = -1e30


def _cparams(sem, flags=None):
    return pltpu.CompilerParams(dimension_semantics=sem, vmem_limit_bytes=VMEM_LIMIT_BYTES, flags=flags)


def _dot(a, b):
    return jnp.dot(a, b, preferred_element_type=F32)


def _dot_nt(a, b):
    return lax.dot_general(a, b, (((1,), (1,)), ((), ())), preferred_element_type=F32)


def _dot_tn(a, b):
    return lax.dot_general(a, b, (((0,), (0,)), ((), ())), preferred_element_type=F32)


def _split_dot(x, w_bf16):
    hi = x.astype(BF16)
    lo = (x - hi.astype(F32)).astype(BF16)
    return _dot(hi, w_bf16) + _dot(lo, w_bf16)


def _softplus(x):
    return jnp.maximum(x, 0.0) + jnp.log1p(jnp.exp(-jnp.abs(x)))


def _gelu_tanh(x):
    return x * (0.5 * (1.0 + jnp.tanh(0.7978845608028654 * (x + 0.044715 * (x * x * x)))))


def _rms_rows(x, g, eps):
    ms = jnp.mean(x * x, axis=-1, keepdims=True)
    return x * lax.rsqrt(ms + eps) * g


def _rms_matmul_kernel(h_ref, g_ref, w_ref, o_ref, hn_ref):
    @pl.when(pl.program_id(1) == 0)
    def _():
        hn_ref[...] = _rms_rows(h_ref[...], g_ref[...], NORM_EPS).astype(BF16)

    o_ref[...] = _dot(hn_ref[...], w_ref[...])


def _rms_matmul(h, g, w, tn=512):
    n, d = h.shape
    n_out = w.shape[1]
    tm = ROW_TILE
    return pl.pallas_call(
        _rms_matmul_kernel,
        grid=(n // tm, n_out // tn),
        in_specs=[
            pl.BlockSpec((tm, d), lambda i, j: (i, 0)),
            pl.BlockSpec((1, d), lambda i, j: (0, 0)),
            pl.BlockSpec((d, tn), lambda i, j: (0, j)),
        ],
        out_specs=pl.BlockSpec((tm, tn), lambda i, j: (i, j)),
        out_shape=jax.ShapeDtypeStruct((n, n_out), F32),
        scratch_shapes=[pltpu.VMEM((tm, d), BF16)],
        compiler_params=_cparams(("parallel", "arbitrary")),
        name="rms_matmul",
    )(h, g.reshape(1, d), w)


def _out_proj_kernel(h_ref, ya_ref, yb_ref, wa_ref, wb_ref, o_ref):
    o_ref[...] = h_ref[...] + _dot(ya_ref[...], wa_ref[...]) + _dot(yb_ref[...], wb_ref[...])


def _out_proj(h, ya, yb, wa, wb, tn=1024):
    n, d = h.shape
    ka, kb = ya.shape[1], yb.shape[1]
    tm = ROW_TILE
    return pl.pallas_call(
        _out_proj_kernel,
        grid=(n // tm, d // tn),
        in_specs=[
            pl.BlockSpec((tm, tn), lambda i, j: (i, j)),
            pl.BlockSpec((tm, ka), lambda i, j: (i, 0)),
            pl.BlockSpec((tm, kb), lambda i, j: (i, 0)),
            pl.BlockSpec((ka, tn), lambda i, j: (0, j)),
            pl.BlockSpec((kb, tn), lambda i, j: (0, j)),
        ],
        out_specs=pl.BlockSpec((tm, tn), lambda i, j: (i, j)),
        out_shape=jax.ShapeDtypeStruct((n, d), F32),
        compiler_params=_cparams(("parallel", "arbitrary")),
        name="out_proj",
    )(h, ya, yb, wa, wb)


def _ffn_kernel(h_ref, g_ref, wg_ref, wu_ref, wd_ref, o_ref, hn_ref, acc_ref):
    j = pl.program_id(1)

    @pl.when(j == 0)
    def _():
        hn_ref[...] = _rms_rows(h_ref[...], g_ref[...], NORM_EPS).astype(BF16)
        acc_ref[...] = jnp.zeros_like(acc_ref)

    hn = hn_ref[...]
    gate = _dot(hn, wg_ref[...])
    up = _dot(hn, wu_ref[...])
    act = (gate * jax.nn.sigmoid(gate) * up).astype(BF16)
    acc_ref[...] += _dot(act, wd_ref[...])

    @pl.when(j == pl.num_programs(1) - 1)
    def _():
        o_ref[...] = h_ref[...] + acc_ref[...]


def _ffn(h, g, wg, wu, wd, th=512):
    n, d = h.shape
    hid = wg.shape[1]
    tm = ROW_TILE
    return pl.pallas_call(
        _ffn_kernel,
        grid=(n // tm, hid // th),
        in_specs=[
            pl.BlockSpec((tm, d), lambda i, j: (i, 0)),
            pl.BlockSpec((1, d), lambda i, j: (0, 0)),
            pl.BlockSpec((d, th), lambda i, j: (0, j)),
            pl.BlockSpec((d, th), lambda i, j: (0, j)),
            pl.BlockSpec((th, d), lambda i, j: (j, 0)),
        ],
        out_specs=pl.BlockSpec((tm, d), lambda i, j: (i, 0)),
        out_shape=jax.ShapeDtypeStruct((n, d), F32),
        scratch_shapes=[pltpu.VMEM((tm, d), BF16), pltpu.VMEM((tm, d), F32)],
        compiler_params=_cparams(("parallel", "arbitrary")),
        name="ffn",
    )(h, g.reshape(1, d), wg, wu, wd)


def _final_norm_kernel(h_ref, g_ref, o_ref):
    o_ref[...] = _rms_rows(h_ref[...], g_ref[...], NORM_EPS)


def _final_norm(h, g):
    n, d = h.shape
    tm = ROW_TILE
    return pl.pallas_call(
        _final_norm_kernel,
        grid=(n // tm,),
        in_specs=[pl.BlockSpec((tm, d), lambda i: (i, 0)), pl.BlockSpec((1, d), lambda i: (0, 0))],
        out_specs=pl.BlockSpec((tm, d), lambda i: (i, 0)),
        out_shape=jax.ShapeDtypeStruct((n, d), F32),
        compiler_params=_cparams(("parallel",)),
        name="final_norm",
    )(h, g.reshape(1, d))


def _lru_kernel(xa_ref, ga_ref, cw_ref, cb_ref, wg_ref, ba_ref, bx_ref, lam_ref, o_ref,
                xe_ref, a_ref, b_ref, hc_ref):
    tt = xa_ref.shape[0]
    t = pl.program_id(1)

    @pl.when(t == 0)
    def _():
        xe_ref[0:SUBLANES, :] = jnp.zeros((SUBLANES, LRU_WIDTH), F32)
        hc_ref[...] = jnp.zeros_like(hc_ref)

    xe_ref[SUBLANES:SUBLANES + tt, :] = xa_ref[...]
    u = cb_ref[...]
    for j in range(CONV_WIDTH):
        off = SUBLANES - (CONV_WIDTH - 1) + j
        u = u + xe_ref[off:off + tt, :] * cw_ref[j:j + 1, :]
    xe_ref[0:SUBLANES, :] = xa_ref[tt - SUBLANES:tt, :]

    sp = _softplus(-lam_ref[...])
    for n in range(LRU_BLOCKS):
        sl = slice(n * LRU_BLOCK, (n + 1) * LRU_BLOCK)
        un = u[:, sl]
        zz = _dot(un.astype(BF16), wg_ref[n])
        r = jax.nn.sigmoid(zz[:, :LRU_BLOCK] + ba_ref[:, sl])
        i = jax.nn.sigmoid(zz[:, LRU_BLOCK:] + bx_ref[:, sl])
        log_a = (-LRU_C) * r * sp[:, sl]
        a = jnp.exp(log_a)
        gain = jnp.sqrt(-jnp.tanh(log_a) * (a * a + 1.0))
        a_ref[:, sl] = a
        b_ref[:, sl] = gain * (i * un)

    row = lax.broadcasted_iota(jnp.int32, (SUBLANES, LRU_WIDTH), 0)

    def body(gi, carry):
        r0 = pl.multiple_of(gi * SUBLANES, SUBLANES)
        a = a_ref[pl.ds(r0, SUBLANES), :]
        b = b_ref[pl.ds(r0, SUBLANES), :]
        for s in (1, 2, 4):
            keep = row >= s
            a_sh = jnp.where(keep, pltpu.roll(a, s, 0), 1.0)
            b_sh = jnp.where(keep, pltpu.roll(b, s, 0), 0.0)
            b = b + a * b_sh
            a = a * a_sh
        hblk = a * carry + b
        b_ref[pl.ds(r0, SUBLANES), :] = hblk
        return hblk[SUBLANES - 1:SUBLANES, :]

    hc_ref[...] = lax.fori_loop(0, tt // SUBLANES, body, hc_ref[...])
    o_ref[...] = (b_ref[...] * _gelu_tanh(ga_ref[...])).astype(BF16)


def _lru_mixer(z, nb, cw, cb, wg, ba, bx, lam, tt=ROW_TILE):
    n = z.shape[0]
    nt = n // nb // tt
    w = LRU_WIDTH
    row = lambda c: pl.BlockSpec((1, w), lambda b, t: (0, 0))
    return pl.pallas_call(
        _lru_kernel,
        grid=(nb, nt),
        in_specs=[
            pl.BlockSpec((tt, w), lambda b, t: (b * nt + t, 0)),
            pl.BlockSpec((tt, w), lambda b, t: (b * nt + t, 1)),
            pl.BlockSpec((CONV_WIDTH, w), lambda b, t: (0, 0)),
            row(0),
            pl.BlockSpec((LRU_BLOCKS, LRU_BLOCK, 2 * LRU_BLOCK), lambda b, t: (0, 0, 0)),
            row(0), row(0), row(0),
        ],
        out_specs=pl.BlockSpec((tt, w), lambda b, t: (b * nt + t, 0)),
        out_shape=jax.ShapeDtypeStruct((n, w), BF16),
        scratch_shapes=[
            pltpu.VMEM((tt + SUBLANES, w), F32),
            pltpu.VMEM((tt, w), F32),
            pltpu.VMEM((tt, w), F32),
            pltpu.VMEM((1, w), F32),
        ],
        compiler_params=_cparams(("parallel", "arbitrary")),
        name="lru_mixer",
    )(z, z, cw, cb.reshape(1, w), wg, ba.reshape(1, w), bx.reshape(1, w), lam.reshape(1, w))


ATT_CB = 256


def _attn_kernel(qi_tab, ki_tab, slopes_ref, q_ref, k_ref, v_ref, posk_ref, lq1_ref, lk1_ref, lq2_ref,
                 lk2_ref, g_ref, o_ref, qaug_ref, kaug_ref, m_ref, acc_ref, *, lambda_init):
    tq = q_ref.shape[0]
    tk = k_ref.shape[0]
    e = DA_HEAD_DIM
    ncb = 2 * tq // ATT_CB
    h = pl.program_id(1)
    p = pl.program_id(2)
    qi = qi_tab[p]
    ki = ki_tab[p]
    slope = slopes_ref[h]

    @pl.when(ki == 0)
    def _():
        q = q_ref[...] * (DA_QK_DIM ** -0.5)
        lane = lax.broadcasted_iota(jnp.int32, (tq, e), 1)
        aug = jnp.where(lane < 2, slope, 0.0)
        rows = [jnp.concatenate([jnp.where(sel, q, 0.0), aug], axis=1)
                for sel in (lane < DA_QK_DIM, lane >= DA_QK_DIM)]
        qboth = jnp.concatenate(rows, axis=0).astype(BF16)
        for cb in range(ncb):
            qaug_ref[cb] = qboth[cb * ATT_CB:(cb + 1) * ATT_CB]
        kaug_ref[:, e:2 * e] = posk_ref[...]
        m_ref[...] = jnp.full_like(m_ref, NEG_BIG)
        acc_ref[...] = jnp.zeros_like(acc_ref)

    off = slope * jnp.full((1, ATT_CB), (ki - qi) * tk, jnp.int32).astype(F32)

    kaug_ref[:, 0:e] = k_ref[...].astype(BF16)
    v_t = jnp.concatenate([v_ref[...].T, jnp.ones((SUBLANES, tk), F32)], axis=0).astype(BF16)

    def run(masked):
        kaug = kaug_ref[...]

        ahead = 1
        pending = [_dot_nt(kaug, qaug_ref[cb]) for cb in range(ahead)]
        for cb in range(ncb):
            s = pending.pop(0)
            if cb + ahead < ncb:
                pending.append(_dot_nt(kaug, qaug_ref[cb + ahead]))
            if masked:
                col = cb * ATT_CB + lax.broadcasted_iota(jnp.int32, (1, ATT_CB), 1)
                ql = jnp.where(col >= tq, col - tq, col)
                kl = lax.broadcasted_iota(jnp.int32, (tk, 1), 0)
                s = jnp.where(kl <= ql, s, NEG_BIG)
            m_prev = m_ref[cb]
            m_new = jnp.maximum(m_prev, jnp.max(s, axis=0, keepdims=True) + off)
            alpha = jnp.exp(m_prev - m_new)
            pm = jnp.exp(s - (m_new - off)).astype(BF16)
            acc_ref[cb] = alpha * acc_ref[cb] + _dot(v_t, pm)
            m_ref[cb] = m_new

    @pl.when(ki != qi)
    def _():
        run(False)

    @pl.when(ki == qi)
    def _():
        run(True)
        lam = (jnp.exp(jnp.sum(lq1_ref[...] * lk1_ref[...], axis=-1, keepdims=True))
               - jnp.exp(jnp.sum(lq2_ref[...] * lk2_ref[...], axis=-1, keepdims=True)) + lambda_init)
        acc = jnp.concatenate([acc_ref[cb, 0:e] for cb in range(ncb)], axis=1)
        den = jnp.concatenate([acc_ref[cb, e:e + 1] for cb in range(ncb)], axis=1)
        o = acc / den
        o = o[:, :tq] - lam * o[:, tq:]
        ms = jnp.mean(o * o, axis=0, keepdims=True)
        o = o * lax.rsqrt(ms + 1e-5) * (g_ref[...] * (1.0 - lambda_init))
        o_ref[...] = o.T.astype(BF16)


def _diff_attention(z, nb, lq1, lk1, lq2, lk2, subln_g, lambda_init, col0, tq=ROW_TILE):
    n = z.shape[0]
    nq = n // nb // tq
    pairs = [(qi, ki) for qi in range(nq) for ki in range(qi + 1)]
    qi_tab = jnp.asarray([p[0] for p in pairs], jnp.int32)
    ki_tab = jnp.asarray([p[1] for p in pairs], jnp.int32)
    slopes = 2.0 ** (-8.0 * jnp.arange(1, DA_HEADS + 1, dtype=F32) / DA_HEADS)
    e = DA_HEAD_DIM
    assert (2 * tq) % ATT_CB == 0 and tq % LANES == 0
    kl = jnp.arange(tq)
    hi_lo = jnp.stack([(kl // LANES) * LANES, kl % LANES], axis=1).astype(F32)
    posk = jnp.pad(hi_lo, ((0, 0), (0, e - 2))).astype(BF16)
    cq, ck, cv = col0 // e, (col0 + DA_WIDTH) // e, (col0 + 2 * DA_WIDTH) // e
    const = lambda shape: pl.BlockSpec(shape, lambda b, h, p, qt, kt: (0, 0))
    grid_spec = pltpu.PrefetchScalarGridSpec(
        num_scalar_prefetch=2,
        grid=(nb, DA_HEADS, len(pairs)),
        in_specs=[
            pl.BlockSpec(memory_space=pltpu.SMEM),
            pl.BlockSpec((tq, e), lambda b, h, p, qt, kt: (b * nq + qt[p], cq + h)),
            pl.BlockSpec((tq, e), lambda b, h, p, qt, kt: (b * nq + kt[p], ck + h)),
            pl.BlockSpec((tq, e), lambda b, h, p, qt, kt: (b * nq + kt[p], cv + h)),
            const((tq, e)),
            const((1, DA_QK_DIM)), const((1, DA_QK_DIM)), const((1, DA_QK_DIM)), const((1, DA_QK_DIM)),
            const((e, 1)),
        ],
        out_specs=pl.BlockSpec((tq, e), lambda b, h, p, qt, kt: (b * nq + qt[p], h)),
        scratch_shapes=[
            pltpu.VMEM((2 * tq // ATT_CB, ATT_CB, 2 * e), BF16),
            pltpu.VMEM((tq, 2 * e), BF16),
            pltpu.VMEM((2 * tq // ATT_CB, 1, ATT_CB), F32),
            pltpu.VMEM((2 * tq // ATT_CB, e + SUBLANES, ATT_CB), F32),
        ],
    )
    r64 = lambda a: a.reshape(1, DA_QK_DIM)
    return pl.pallas_call(
        functools.partial(_attn_kernel, lambda_init=lambda_init),
        grid_spec=grid_spec,
        out_shape=jax.ShapeDtypeStruct((n, DA_WIDTH), BF16),
        compiler_params=_cparams(("parallel", "parallel", "arbitrary")),
        name="diff_attention",
    )(qi_tab, ki_tab, slopes, z, z, z, posk, r64(lq1), r64(lk1), r64(lq2), r64(lk2),
      subln_g.reshape(e, 1))


def _s5_disc_kernel(lr_ref, li_ref, ldt_ref, ar_ref, ai_ref, cr_ref, ci_ref):
    lr, li = lr_ref[...], li_ref[...]
    dt = jnp.exp(ldt_ref[...])
    mag = jnp.exp(lr * dt)
    ar = mag * jnp.cos(li * dt)
    ai = mag * jnp.sin(li * dt)
    den = lr * lr + li * li
    ar_ref[...] = ar
    ai_ref[...] = ai
    cr_ref[...] = ((ar - 1.0) * lr + ai * li) / den
    ci_ref[...] = (ai * lr - (ar - 1.0) * li) / den


def _s5_discretise(lam_re, lam_im, log_dt):
    shp = jax.ShapeDtypeStruct(lam_re.shape, F32)
    ldt = jnp.broadcast_to(log_dt[:, None], lam_re.shape)
    return pl.pallas_call(_s5_disc_kernel, out_shape=(shp, shp, shp, shp), name="s5_discretise")(
        lam_re, lam_im, ldt)


S5_KB = 4
S5_KB_CH = S5_WIDTH // S5_KB
S5_KB_ST = S5_GROUPS // S5_KB * S5_STATE


def _s5_kernel(u_ref, wb_ref, wc_ref, ar_ref, ai_ref, d_ref, gw_ref, gb_ref, o_ref,
               x_ref, pr_ref, pi_ref, cr_ref, ci_ref, y_ref):
    tt = u_ref.shape[0]
    t = pl.program_id(1)
    ns = S5_KB_ST

    @pl.when(t == 0)
    def _():
        cr_ref[...] = jnp.zeros_like(cr_ref)
        ci_ref[...] = jnp.zeros_like(ci_ref)
        for kb in range(S5_KB):
            ar, ai = ar_ref[kb:kb + 1, :], ai_ref[kb:kb + 1, :]
            pr, pi = ar, ai
            for j in range(SUBLANES):
                pr_ref[kb, j:j + 1, :] = pr
                pi_ref[kb, j:j + 1, :] = pi
                pr, pi = pr * ar - pi * ai, pr * ai + pi * ar

    u = u_ref[...]
    for kb in range(S5_KB):
        ukb = u[:, kb * S5_KB_CH:(kb + 1) * S5_KB_CH].astype(BF16)
        x_ref[kb] = _dot(ukb, wb_ref[kb])

    row = lax.broadcasted_iota(jnp.int32, (SUBLANES, ns), 0)

    def body(gi, carry):
        r0 = pl.multiple_of(gi * SUBLANES, SUBLANES)
        new = []
        for kb in range(S5_KB):
            cr, ci = carry[2 * kb], carry[2 * kb + 1]
            xr = x_ref[kb, pl.ds(r0, SUBLANES), 0:ns]
            xi = x_ref[kb, pl.ds(r0, SUBLANES), ns:2 * ns]
            for s in (1, 2, 4):
                keep = row >= s
                ar = pr_ref[kb, s - 1:s, :]
                ai = pi_ref[kb, s - 1:s, :]
                sr = jnp.where(keep, pltpu.roll(xr, s, 0), 0.0)
                si = jnp.where(keep, pltpu.roll(xi, s, 0), 0.0)
                xr, xi = xr + ar * sr - ai * si, xi + ar * si + ai * sr
            pr, pi = pr_ref[kb], pi_ref[kb]
            xr, xi = xr + pr * cr - pi * ci, xi + pr * ci + pi * cr
            x_ref[kb, pl.ds(r0, SUBLANES), 0:ns] = xr
            x_ref[kb, pl.ds(r0, SUBLANES), ns:2 * ns] = xi
            new += [xr[SUBLANES - 1:SUBLANES, :], xi[SUBLANES - 1:SUBLANES, :]]
        return tuple(new)

    init = []
    for kb in range(S5_KB):
        init += [cr_ref[kb:kb + 1, :], ci_ref[kb:kb + 1, :]]
    fin = lax.fori_loop(0, tt // SUBLANES, body, tuple(init))
    for kb in range(S5_KB):
        cr_ref[kb:kb + 1, :] = fin[2 * kb]
        ci_ref[kb:kb + 1, :] = fin[2 * kb + 1]

    for kb in range(S5_KB):
        y_ref[:, kb * S5_KB_CH:(kb + 1) * S5_KB_CH] = _dot(x_ref[kb].astype(BF16), wc_ref[kb])
    y = _gelu_tanh(y_ref[...] + d_ref[...] * u)
    gate = jax.nn.sigmoid(_dot(y.astype(BF16), gw_ref[...]) + gb_ref[...])
    o_ref[...] = (y * gate).astype(BF16)


def _s5_mixer(z, nb, wb, wc, ar, ai, d_skip, glu_w, glu_b, tt=128):
    n = z.shape[0]
    nt = n // nb // tt
    w = S5_WIDTH
    full = lambda shape: pl.BlockSpec(shape, lambda b, t: (0,) * len(shape))
    return pl.pallas_call(
        _s5_kernel,
        grid=(nb, nt),
        in_specs=[
            pl.BlockSpec((tt, w), lambda b, t: (b * nt + t, 0)),
            full((S5_KB, S5_KB_CH, 2 * S5_KB_ST)),
            full((S5_KB, 2 * S5_KB_ST, S5_KB_CH)),
            full((S5_KB, S5_KB_ST)),
            full((S5_KB, S5_KB_ST)),
            full((1, w)),
            full((w, w)),
            full((1, w)),
        ],
        out_specs=pl.BlockSpec((tt, w), lambda b, t: (b * nt + t, 0)),
        out_shape=jax.ShapeDtypeStruct((n, w), BF16),
        scratch_shapes=[
            pltpu.VMEM((S5_KB, tt, 2 * S5_KB_ST), F32),
            pltpu.VMEM((S5_KB, SUBLANES, S5_KB_ST), F32),
            pltpu.VMEM((S5_KB, SUBLANES, S5_KB_ST), F32),
            pltpu.VMEM((S5_KB, S5_KB_ST), F32),
            pltpu.VMEM((S5_KB, S5_KB_ST), F32),
            pltpu.VMEM((tt, w), F32),
        ],
        compiler_params=_cparams(("parallel", "arbitrary")),
        name="s5_mixer",
    )(z, wb, wc, ar, ai, d_skip.reshape(1, w), glu_w, glu_b.reshape(1, w))


def _token_shift(x_ref, mu_ref, xe_ref, first):
    tt = x_ref.shape[0]

    @pl.when(first)
    def _():
        xe_ref[0:SUBLANES, :] = jnp.zeros((SUBLANES, x_ref.shape[1]), F32)

    x = x_ref[...]
    xe_ref[SUBLANES:SUBLANES + tt, :] = x
    xprev = xe_ref[SUBLANES - 1:SUBLANES - 1 + tt, :]
    xe_ref[0:SUBLANES, :] = x_ref[tt - SUBLANES:tt, :]
    return x + (xprev - x) * mu_ref[...]


def _rw_prep_kernel(xr_ref, xk_ref, xv_ref, xl_ref, mur_ref, muk_ref, muv_ref, mul_ref,
                    w0_ref, w2_ref, a0_ref, a2_ref, g2_ref, kk_ref, ka_ref, ones_ref,
                    r_out, lw_out, k_out, v_out, kk_out, a_out, g_out,
                    er_ref, ek_ref, ev_ref, el_ref):
    first = pl.program_id(1) == 0
    r = _token_shift(xr_ref, mur_ref, er_ref, first)
    k = _token_shift(xk_ref, muk_ref, ek_ref, first)
    v = _token_shift(xv_ref, muv_ref, ev_ref, first)
    lo = _token_shift(xl_ref, mul_ref, el_ref, first)
    wl = lo[:, 0:RW_LORA_PAD]
    al = lo[:, RW_LORA_PAD:2 * RW_LORA_PAD]
    gl = lo[:, 2 * RW_LORA_PAD:2 * RW_LORA_PAD + RW_GATE_LORA]

    wdec = -_softplus(-(w0_ref[...] + _dot(jnp.tanh(wl).astype(BF16), w2_ref[...]))) - 0.5
    a = jax.nn.sigmoid(a0_ref[...] + _dot(al.astype(BF16), a2_ref[...]))
    g = _dot(jax.nn.sigmoid(gl).astype(BF16), g2_ref[...])
    kkr = k * kk_ref[...]
    ssq = _split_dot(kkr * kkr, ones_ref[...])
    kk = kkr / jnp.maximum(jnp.sqrt(ssq), 1e-12)

    r_out[...] = r
    lw_out[...] = -jnp.exp(wdec)
    k_out[...] = k * (1.0 + (a - 1.0) * ka_ref[...])
    v_out[...] = v
    kk_out[...] = kk
    a_out[...] = a
    g_out[...] = g


RW_LORA_COLS = 2 * RW_LORA_PAD + RW_GATE_LORA


def _rw_prep(z, nb, col0, mu, w0, w2, a0, a2, g2, k_k, k_a, ones_bd, tt=320):
    n = z.shape[0]
    nt = n // nb // tt
    w = RW_WIDTH
    lw = RW_LORA_COLS
    assert col0 % w == 0 and (col0 + 3 * w) % lw == 0
    cb = col0 // w
    cl = (col0 + 3 * w) // lw
    full = lambda shape: pl.BlockSpec(shape, lambda b, t: (0,) * len(shape))
    xblk = lambda width, c: pl.BlockSpec((tt, width), lambda b, t: (b * nt + t, c))
    out = jax.ShapeDtypeStruct((n, w), F32)
    ospec = pl.BlockSpec((tt, w), lambda b, t: (b * nt + t, 0))
    mu = mu.reshape(1, 3 * w + lw)
    return pl.pallas_call(
        _rw_prep_kernel,
        grid=(nb, nt),
        in_specs=[
            xblk(w, cb), xblk(w, cb + 1), xblk(w, cb + 2), xblk(lw, cl),
            full((1, w)), full((1, w)), full((1, w)), full((1, lw)),
            full((1, w)), full((RW_LORA_PAD, w)), full((1, w)), full((RW_LORA_PAD, w)),
            full((RW_GATE_LORA, w)), full((1, w)), full((1, w)), full((w, w)),
        ],
        out_specs=[ospec] * 7,
        out_shape=[out] * 7,
        scratch_shapes=[pltpu.VMEM((tt + SUBLANES, w), F32)] * 3 + [pltpu.VMEM((tt + SUBLANES, lw), F32)],
        compiler_params=_cparams(("parallel", "arbitrary")),
        name="rwkv_prep",
    )(z, z, z, z, mu[:, 0:w], mu[:, w:2 * w], mu[:, 2 * w:3 * w], mu[:, 3 * w:],
      w0.reshape(1, w), w2, a0.reshape(1, w), a2, g2, k_k.reshape(1, w), k_a.reshape(1, w), ones_bd)


def _rw_chunk_kernel(r_ref, lw_ref, k_ref, v_ref, kk_ref, a_ref, g_ref, rk_ref, lnw_ref, lnb_ref,
                     ones_ref, o_ref, s_ref):
    @pl.when(pl.program_id(1) == 0)
    def _():
        s_ref[...] = jnp.zeros_like(s_ref)

    gw = RW_GROUP * RW_HEAD_DIM
    nb, _, width = r_ref.shape
    chains = [(b, slice(j * gw, (j + 1) * gw)) for b in range(nb) for j in range(width // gw)]
    ins = [tuple(ref[b, :, sl] for ref in (r_ref, lw_ref, k_ref, v_ref, kk_ref, a_ref))
           for b, sl in chains]
    s0s = [s_ref[i] for i in range(len(chains))]
    ys, s_news = _rw_chunk_steps(ins, s0s)
    for i, (b, sl) in enumerate(chains):
        s_ref[i] = s_news[i]
        o_ref[b, :, sl] = _rw_output(ys[i], r_ref[b, :, sl], k_ref[b, :, sl], v_ref[b, :, sl],
                                     g_ref[b, :, sl], rk_ref[:, sl], lnw_ref[:, sl], lnb_ref[:, sl],
                                     ones_ref[...])


def _rw_chunk_prepare(r, lw, k, v, kk, a):
    L = RW_CHUNK
    gw = RW_GROUP * RW_HEAD_DIM

    ti = lax.broadcasted_iota(jnp.int32, (L, L), 0)
    tj = lax.broadcasted_iota(jnp.int32, (L, L), 1)
    tri = jnp.where(ti >= tj, 1.0, 0.0).astype(BF16)
    x1 = lw.astype(BF16)
    r1 = lw - x1.astype(F32)
    x2 = r1.astype(BF16)
    x3 = (r1 - x2.astype(F32)).astype(BF16)
    cl = _dot(tri, x1) + _dot(tri, x2) + _dot(tri, x3)
    g_last = jnp.exp(cl[L - 1:L, :])
    e_in = jnp.exp(cl)
    e_ex = jnp.exp(cl - lw)
    e_inv = jnp.exp(-cl)
    beta = kk * a
    a_t = -kk * e_ex
    r_t = r * e_in
    b_t = beta * e_inv
    k_t = k * e_inv

    lane_head = lax.shift_right_logical(lax.broadcasted_iota(jnp.int32, (L, gw), 1),
                                        int(math.log2(RW_HEAD_DIM)))

    def stack(x):
        return jnp.concatenate([jnp.where(lane_head == hh, x, 0.0) for hh in range(RW_GROUP)],
                               axis=0).astype(BF16)

    ar_s = jnp.concatenate([stack(a_t), stack(r_t)], axis=0)
    bk_s = jnp.concatenate([stack(b_t), stack(k_t)], axis=0)
    return dict(ar_s=ar_s, bk_s=bk_s, v_s=stack(v), bh_s=stack(b_t * g_last), kh_s=stack(k_t * g_last),
                g_last=g_last)


def _rw_chunk_steps(ins, s0s):
    L = RW_CHUNK
    rows = RW_GROUP * L
    n = range(len(ins))
    pre = [_rw_chunk_prepare(*x) for x in ins]
    prod = [_dot_nt(p["ar_s"], p["bk_s"]) for p in pre]
    w0 = [_dot_nt(p["ar_s"], s0.astype(BF16)) for p, s0 in zip(pre, s0s)]
    ri = lax.broadcasted_iota(jnp.int32, (rows, rows), 0)
    ci = lax.broadcasted_iota(jnp.int32, (rows, rows), 1)
    strict = ri > ci
    incl = ri >= ci
    npow = [jnp.where(strict, pr[:rows, :rows], 0.0).astype(BF16) for pr in prod]
    a_ak = [jnp.where(strict, pr[:rows, rows:], 0.0).astype(BF16) for pr in prod]
    m_rb = [jnp.where(incl, pr[rows:, :rows], 0.0).astype(BF16) for pr in prod]
    m_rk = [jnp.where(incl, pr[rows:, rows:], 0.0).astype(BF16) for pr in prod]
    x = [w0[i][:rows] + _dot(a_ak[i], pre[i]["v_s"]) for i in n]
    steps = int(math.log2(L))
    for kx in range(steps):
        x = [x[i] + _dot(npow[i], x[i].astype(BF16)) for i in n]
        if kx + 1 < steps:
            npow = [_dot(npow[i], npow[i]).astype(BF16) for i in n]
    u_s = [xi.astype(BF16) for xi in x]
    y_s = [w0[i][rows:] + _dot(m_rb[i], u_s[i]) + _dot(m_rk[i], pre[i]["v_s"]) for i in n]
    ys = []
    for ysi in y_s:
        y = ysi[0:L]
        for hh in range(1, RW_GROUP):
            y = y + ysi[hh * L:(hh + 1) * L]
        ys.append(y)
    s_new = [s0s[i] * pre[i]["g_last"] + _dot_tn(u_s[i], pre[i]["bh_s"])
             + _dot_tn(pre[i]["v_s"], pre[i]["kh_s"]) for i in n]
    return ys, s_new


def _rw_output(y, r, k, v, g, r_k, ln_w, ln_b, ones):
    inv_n = 1.0 / RW_HEAD_DIM
    mean = _split_dot(y, ones) * inv_n
    yc = y - mean
    var = _split_dot(yc * yc, ones) * inv_n
    yn = yc * lax.rsqrt(var + RW_GN_EPS) * ln_w + ln_b
    bonus = _split_dot(r * k * r_k, ones) * v
    return ((yn + bonus) * g).astype(BF16)


def _rw_chunk(prep, nb, r_k, ln_w, ln_b, ones_g):
    n = prep[0].shape[0]
    L = RW_CHUNK
    gw = RW_GROUP * RW_HEAD_DIM
    bw = RW_STEP_GROUPS * gw
    tp = n // nb
    w = RW_WIDTH
    blk = lambda: pl.BlockSpec((nb, L, bw), lambda gi, c: (0, c, gi))
    par = lambda: pl.BlockSpec((1, bw), lambda gi, c: (0, gi))
    out = pl.pallas_call(
        _rw_chunk_kernel,
        grid=(w // bw, tp // L),
        in_specs=[blk() for _ in range(7)] + [par(), par(), par(),
                                             pl.BlockSpec((gw, gw), lambda gi, c: (0, 0))],
        out_specs=blk(),
        out_shape=jax.ShapeDtypeStruct((nb, tp, w), BF16),
        scratch_shapes=[pltpu.VMEM((nb * RW_STEP_GROUPS, gw, gw), F32)],
        compiler_params=_cparams(("parallel", "arbitrary")),
        name="rwkv_chunk",
    )(*[x.reshape(nb, tp, w) for x in prep], r_k.reshape(1, w), ln_w.reshape(1, w), ln_b.reshape(1, w),
      ones_g)
    return out.reshape(n, w)


def _head_ones(width, head):
    idx = jnp.arange(width) // head
    return (idx[:, None] == idx[None, :]).astype(BF16)


def _pack_lru_gates(wa, wx):
    return jnp.concatenate([wa, wx], axis=-1).astype(BF16)


def _pack_odd_in(w_in, mu):
    s5 = S5_WIDTH
    c_rkv = s5 + 3 * RW_WIDTH
    c_wl = c_rkv + RW_DECAY_LORA
    c_al = c_wl + RW_AAA_LORA
    padw = lambda x, n: jnp.pad(x, ((0, 0), (0, n - x.shape[1])))
    w = jnp.concatenate([
        w_in[:, :c_rkv],
        padw(w_in[:, c_rkv:c_wl], RW_LORA_PAD),
        padw(w_in[:, c_wl:c_al], RW_LORA_PAD),
        w_in[:, c_al:],
    ], axis=1)
    m = mu[None, :]
    o = s5
    mu_p = jnp.concatenate([
        m[:, :c_rkv - o],
        padw(m[:, c_rkv - o:c_wl - o], RW_LORA_PAD),
        padw(m[:, c_wl - o:c_al - o], RW_LORA_PAD),
        m[:, c_al - o:],
    ], axis=1)[0]
    return w.astype(BF16), mu_p


def _pad_rows(x, n):
    return jnp.pad(x, ((0, n - x.shape[0]), (0, 0)))


def _pack_s5(cr, ci, b_re, b_im, c_re, c_im):
    gpb = S5_GROUPS // S5_KB
    bbr = cr[..., None] * b_re - ci[..., None] * b_im
    bbi = cr[..., None] * b_im + ci[..., None] * b_re
    eye = jnp.eye(gpb, dtype=F32)

    def in_proj(bb):
        x = bb.reshape(S5_KB, gpb, S5_STATE, S5_GROUP)
        return jnp.einsum("kgpc,gh->kgchp", x, eye).reshape(S5_KB, S5_KB_CH, S5_KB_ST)

    def out_proj(cc):
        x = cc.reshape(S5_KB, gpb, S5_GROUP, S5_STATE)
        return jnp.einsum("kgcp,gh->kgphc", x, eye).reshape(S5_KB, S5_KB_ST, S5_KB_CH)

    wb = jnp.concatenate([in_proj(bbr), in_proj(bbi)], axis=2).astype(BF16)
    wc = jnp.concatenate([out_proj(c_re), out_proj(-c_im)], axis=1).astype(BF16)
    return wb, wc


def kernel(x, meta_tokens, norm_mix_g, norm_ffn_g, final_norm_g, ev_w_in, ev_conv_w, ev_conv_b, ev_lru_wa, ev_lru_ba, ev_lru_wx, ev_lru_bx, ev_lru_lambda, ev_lq1, ev_lk1, ev_lq2, ev_lk2, ev_subln_g, ev_w_out, od_w_in, od_s5_lam_re, od_s5_lam_im, od_s5_log_dt, od_s5_b_re, od_s5_b_im, od_s5_c_re, od_s5_c_im, od_s5_d, od_glu_w, od_glu_b, od_rw_mu, od_rw_w0, od_rw_w2, od_rw_a0, od_rw_a2, od_rw_g2, od_rw_kk, od_rw_ka, od_rw_rk, od_rw_ln_w, od_rw_ln_b, od_w_out, ffn_w_gate, ffn_w_up, ffn_w_down):
    nb, seq, d = x.shape
    depth = norm_mix_g.shape[0]
    t_real = N_META + seq
    tp = -(-t_real // ROW_TILE) * ROW_TILE
    meta = jnp.broadcast_to(meta_tokens[None].astype(x.dtype), (nb, N_META, d))
    pad = jnp.zeros((nb, tp - t_real, d), x.dtype)
    h = jnp.concatenate([meta, x, pad], axis=1).reshape(nb * tp, d)

    ones_rw = _head_ones(RW_WIDTH, RW_HEAD_DIM)
    ones_grp = _head_ones(RW_GROUP * RW_HEAD_DIM, RW_HEAD_DIM)

    for layer in range(depth):
        j = layer // 2
        if layer % 2 == 0:
            lambda_init = 0.8 - 0.6 * math.exp(-0.3 * layer)
            z = _rms_matmul(h, norm_mix_g[layer], ev_w_in[j].astype(BF16), tn=1280)
            ya = _lru_mixer(z, nb, ev_conv_w[j], ev_conv_b[j], _pack_lru_gates(ev_lru_wa[j], ev_lru_wx[j]),
                            ev_lru_ba[j], ev_lru_bx[j], ev_lru_lambda[j])
            yb = _diff_attention(z, nb, ev_lq1[j], ev_lk1[j], ev_lq2[j], ev_lk2[j], ev_subln_g[j],
                                 lambda_init, col0=2 * LRU_WIDTH)
            w_out = ev_w_out[j].astype(BF16)
            h = _out_proj(h, ya, yb, w_out[:LRU_WIDTH], w_out[LRU_WIDTH:])
        else:
            w_in, mu_p = _pack_odd_in(od_w_in[j], od_rw_mu[j])
            z = _rms_matmul(h, norm_mix_g[layer], w_in, tn=1536)
            ar, ai, cr, ci = _s5_discretise(od_s5_lam_re[j], od_s5_lam_im[j], od_s5_log_dt[j])
            wb, wc = _pack_s5(cr, ci, od_s5_b_re[j], od_s5_b_im[j], od_s5_c_re[j], od_s5_c_im[j])
            yc = _s5_mixer(z, nb, wb, wc, ar.reshape(S5_KB, S5_KB_ST), ai.reshape(S5_KB, S5_KB_ST),
                           od_s5_d[j], od_glu_w[j].astype(BF16), od_glu_b[j])
            prep = _rw_prep(z, nb, S5_WIDTH, mu_p, od_rw_w0[j],
                            _pad_rows(od_rw_w2[j], RW_LORA_PAD).astype(BF16), od_rw_a0[j],
                            _pad_rows(od_rw_a2[j], RW_LORA_PAD).astype(BF16), od_rw_g2[j].astype(BF16),
                            od_rw_kk[j], od_rw_ka[j], ones_rw)
            yd = _rw_chunk(prep, nb, od_rw_rk[j], od_rw_ln_w[j], od_rw_ln_b[j], ones_grp)
            w_out = od_w_out[j].astype(BF16)
            h = _out_proj(h, yc, yd, w_out[:S5_WIDTH], w_out[S5_WIDTH:])
        h = _ffn(h, norm_ffn_g[layer], ffn_w_gate[layer].astype(BF16), ffn_w_up[layer].astype(BF16),
                 ffn_w_down[layer].astype(BF16))
    out = _final_norm(h, final_norm_g).reshape(nb, tp, d)
    return out[:, N_META:t_real]
```

```python
import functools
import math

import jax
import jax.numpy as jnp
from jax import lax
from jax.experimental import pallas as pl
from jax.experimental.pallas import tpu as pltpu

F32 = jnp.float32
BF16 = jnp.bfloat16

D_MODEL = 2048
N_META = 16
NORM_EPS = 1e-6
LRU_WIDTH = 1024
LRU_BLOCKS = 8
LRU_BLOCK = 128
CONV_WIDTH = 4
LRU_C = 8.0
DA_WIDTH = 1024
DA_HEADS = 8
DA_HEAD_DIM = 128
DA_QK_DIM = 64
S5_WIDTH = 1024
S5_GROUP = 16
S5_GROUPS = 64
S5_STATE = 64
RW_WIDTH = 1024
RW_HEAD_DIM = 64
RW_HEADS = 16
RW_GN_EPS = 64e-5
RW_DECAY_LORA = 96
RW_AAA_LORA = 96
RW_GATE_LORA = 256
RW_LORA_PAD = 128
FFN_HIDDEN = 5632

SUBLANES = 8
LANES = 128
VMEM_LIMIT_BYTES = 56 * 1024 * 1024

ROW_TILE = 640
RW_CHUNK = 64
RW_GROUP = 4
RW_STEP_GROUPS = 2
NEG_BIG = -1e30


def _cparams(sem, flags=None):
    return pltpu.CompilerParams(dimension_semantics=sem, vmem_limit_bytes=VMEM_LIMIT_BYTES, flags=flags)


def _dot(a, b):
    return jnp.dot(a, b, preferred_element_type=F32)


def _dot_nt(a, b):
    return lax.dot_general(a, b, (((1,), (1,)), ((), ())), preferred_element_type=F32)


def _dot_tn(a, b):
    return lax.dot_general(a, b, (((0,), (0,)), ((), ())), preferred_element_type=F32)


def _split_dot(x, w_bf16):
    hi = x.astype(BF16)
    lo = (x - hi.astype(F32)).astype(BF16)
    return _dot(hi, w_bf16) + _dot(lo, w_bf16)


def _softplus(x):
    return jnp.maximum(x, 0.0) + jnp.log1p(jnp.exp(-jnp.abs(x)))


def _gelu_tanh(x):
    return x * (0.5 * (1.0 + jnp.tanh(0.7978845608028654 * (x + 0.044715 * (x * x * x)))))


def _rms_rows(x, g, eps):
    ms = jnp.mean(x * x, axis=-1, keepdims=True)
    return x * lax.rsqrt(ms + eps) * g


def _rms_matmul_kernel(h_ref, g_ref, w_ref, o_ref, hn_ref):
    @pl.when(pl.program_id(1) == 0)
    def _():
        hn_ref[...] = _rms_rows(h_ref[...], g_ref[...], NORM_EPS).astype(BF16)

    o_ref[...] = _dot(hn_ref[...], w_ref[...])


def _rms_matmul(h, g, w, tn=512):
    n, d = h.shape
    n_out = w.shape[1]
    tm = ROW_TILE
    return pl.pallas_call(
        _rms_matmul_kernel,
        grid=(n // tm, n_out // tn),
        in_specs=[
            pl.BlockSpec((tm, d), lambda i, j: (i, 0)),
            pl.BlockSpec((1, d), lambda i, j: (0, 0)),
            pl.BlockSpec((d, tn), lambda i, j: (0, j)),
        ],
        out_specs=pl.BlockSpec((tm, tn), lambda i, j: (i, j)),
        out_shape=jax.ShapeDtypeStruct((n, n_out), F32),
        scratch_shapes=[pltpu.VMEM((tm, d), BF16)],
        compiler_params=_cparams(("parallel", "arbitrary")),
        name="rms_matmul",
    )(h, g.reshape(1, d), w)


def _out_proj_kernel(h_ref, ya_ref, yb_ref, wa_ref, wb_ref, o_ref):
    o_ref[...] = h_ref[...] + _dot(ya_ref[...], wa_ref[...]) + _dot(yb_ref[...], wb_ref[...])


def _out_proj(h, ya, yb, wa, wb, tn=1024):
    n, d = h.shape
    ka, kb = ya.shape[1], yb.shape[1]
    tm = ROW_TILE
    return pl.pallas_call(
        _out_proj_kernel,
        grid=(n // tm, d // tn),
        in_specs=[
            pl.BlockSpec((tm, tn), lambda i, j: (i, j)),
            pl.BlockSpec((tm, ka), lambda i, j: (i, 0)),
            pl.BlockSpec((tm, kb), lambda i, j: (i, 0)),
            pl.BlockSpec((ka, tn), lambda i, j: (0, j)),
            pl.BlockSpec((kb, tn), lambda i, j: (0, j)),
        ],
        out_specs=pl.BlockSpec((tm, tn), lambda i, j: (i, j)),
        out_shape=jax.ShapeDtypeStruct((n, d), F32),
        compiler_params=_cparams(("parallel", "arbitrary")),
        name="out_proj",
    )(h, ya, yb, wa, wb)


def _ffn_kernel(h_ref, g_ref, wg_ref, wu_ref, wd_ref, o_ref, hn_ref, acc_ref):
    j = pl.program_id(1)

    @pl.when(j == 0)
    def _():
        hn_ref[...] = _rms_rows(h_ref[...], g_ref[...], NORM_EPS).astype(BF16)
        acc_ref[...] = jnp.zeros_like(acc_ref)

    hn = hn_ref[...]
    gate = _dot(hn, wg_ref[...])
    up = _dot(hn, wu_ref[...])
    act = (gate * jax.nn.sigmoid(gate) * up).astype(BF16)
    acc_ref[...] += _dot(act, wd_ref[...])

    @pl.when(j == pl.num_programs(1) - 1)
    def _():
        o_ref[...] = h_ref[...] + acc_ref[...]


def _ffn(h, g, wg, wu, wd, th=512):
    n, d = h.shape
    hid = wg.shape[1]
    tm = ROW_TILE
    return pl.pallas_call(
        _ffn_kernel,
        grid=(n // tm, hid // th),
        in_specs=[
            pl.BlockSpec((tm, d), lambda i, j: (i, 0)),
            pl.BlockSpec((1, d), lambda i, j: (0, 0)),
            pl.BlockSpec((d, th), lambda i, j: (0, j)),
            pl.BlockSpec((d, th), lambda i, j: (0, j)),
            pl.BlockSpec((th, d), lambda i, j: (j, 0)),
        ],
        out_specs=pl.BlockSpec((tm, d), lambda i, j: (i, 0)),
        out_shape=jax.ShapeDtypeStruct((n, d), F32),
        scratch_shapes=[pltpu.VMEM((tm, d), BF16), pltpu.VMEM((tm, d), F32)],
        compiler_params=_cparams(("parallel", "arbitrary")),
        name="ffn",
    )(h, g.reshape(1, d), wg, wu, wd)


def _final_norm_kernel(h_ref, g_ref, o_ref):
    o_ref[...] = _rms_rows(h_ref[...], g_ref[...], NORM_EPS)


def _final_norm(h, g, nb, seq, tm=512):
    n, d = h.shape
    tp = n // nb
    nt = seq // tm
    assert nt * tm == seq and N_META % SUBLANES == 0
    out = pl.pallas_call(
        _final_norm_kernel,
        grid=(nb, nt),
        in_specs=[pl.BlockSpec((pl.Element(tm), pl.Element(d)),
                               lambda b, i: (pl.multiple_of(b * tp + N_META + i * tm, SUBLANES), 0)),
                  pl.BlockSpec((1, d), lambda b, i: (0, 0))],
        out_specs=pl.BlockSpec((tm, d), lambda b, i: (b * nt + i, 0)),
        out_shape=jax.ShapeDtypeStruct((nb * seq, d), F32),
        compiler_params=_cparams(("parallel", "parallel")),
        name="final_norm",
    )(h, g.reshape(1, d))
    return out.reshape(nb, seq, d)


def _lru_kernel(xa_ref, ga_ref, cw_ref, cb_ref, wg_ref, ba_ref, bx_ref, lam_ref, o_ref,
                xe_ref, a_ref, b_ref, hc_ref):
    tt = xa_ref.shape[0]
    t = pl.program_id(1)

    @pl.when(t == 0)
    def _():
        xe_ref[0:SUBLANES, :] = jnp.zeros((SUBLANES, LRU_WIDTH), F32)
        hc_ref[...] = jnp.zeros_like(hc_ref)

    xe_ref[SUBLANES:SUBLANES + tt, :] = xa_ref[...]
    u = cb_ref[...]
    for j in range(CONV_WIDTH):
        off = SUBLANES - (CONV_WIDTH - 1) + j
        u = u + xe_ref[off:off + tt, :] * cw_ref[j:j + 1, :]
    xe_ref[0:SUBLANES, :] = xa_ref[tt - SUBLANES:tt, :]

    sp = _softplus(-lam_ref[...])
    for n in range(LRU_BLOCKS):
        sl = slice(n * LRU_BLOCK, (n + 1) * LRU_BLOCK)
        un = u[:, sl]
        zz = _dot(un.astype(BF16), wg_ref[n])
        r = jax.nn.sigmoid(zz[:, :LRU_BLOCK] + ba_ref[:, sl])
        i = jax.nn.sigmoid(zz[:, LRU_BLOCK:] + bx_ref[:, sl])
        log_a = (-LRU_C) * r * sp[:, sl]
        a = jnp.exp(log_a)
        gain = jnp.sqrt(-jnp.tanh(log_a) * (a * a + 1.0))
        a_ref[:, sl] = a
        b_ref[:, sl] = gain * (i * un)

    row = lax.broadcasted_iota(jnp.int32, (SUBLANES, LRU_WIDTH), 0)

    def body(gi, carry):
        r0 = pl.multiple_of(gi * SUBLANES, SUBLANES)
        a = a_ref[pl.ds(r0, SUBLANES), :]
        b = b_ref[pl.ds(r0, SUBLANES), :]
        for s in (1, 2, 4):
            keep = row >= s
            a_sh = jnp.where(keep, pltpu.roll(a, s, 0), 1.0)
            b_sh = jnp.where(keep, pltpu.roll(b, s, 0), 0.0)
            b = b + a * b_sh
            a = a * a_sh
        hblk = a * carry + b
        b_ref[pl.ds(r0, SUBLANES), :] = hblk
        return hblk[SUBLANES - 1:SUBLANES, :]

    hc_ref[...] = lax.fori_loop(0, tt // SUBLANES, body, hc_ref[...])
    o_ref[...] = (b_ref[...] * _gelu_tanh(ga_ref[...])).astype(BF16)


def _lru_mixer(z, nb, cw, cb, wg, ba, bx, lam, tt=ROW_TILE):
    n = z.shape[0]
    nt = n // nb // tt
    w = LRU_WIDTH
    row = lambda c: pl.BlockSpec((1, w), lambda b, t: (0, 0))
    return pl.pallas_call(
        _lru_kernel,
        grid=(nb, nt),
        in_specs=[
            pl.BlockSpec((tt, w), lambda b, t: (b * nt + t, 0)),
            pl.BlockSpec((tt, w), lambda b, t: (b * nt + t, 1)),
            pl.BlockSpec((CONV_WIDTH, w), lambda b, t: (0, 0)),
            row(0),
            pl.BlockSpec((LRU_BLOCKS, LRU_BLOCK, 2 * LRU_BLOCK), lambda b, t: (0, 0, 0)),
            row(0), row(0), row(0),
        ],
        out_specs=pl.BlockSpec((tt, w), lambda b, t: (b * nt + t, 0)),
        out_shape=jax.ShapeDtypeStruct((n, w), BF16),
        scratch_shapes=[
            pltpu.VMEM((tt + SUBLANES, w), F32),
            pltpu.VMEM((tt, w), F32),
            pltpu.VMEM((tt, w), F32),
            pltpu.VMEM((1, w), F32),
        ],
        compiler_params=_cparams(("parallel", "arbitrary")),
        name="lru_mixer",
    )(z, z, cw, cb.reshape(1, w), wg, ba.reshape(1, w), bx.reshape(1, w), lam.reshape(1, w))


ATT_CB = 256


def _attn_prep_kernel(k_ref, v_ref, posk_ref, kaug_ref, vt_ref):
    e = DA_HEAD_DIM
    tk = k_ref.shape[0]
    ones = jnp.ones((SUBLANES, tk), F32)
    for hh in range(DA_HEADS):
        sl = slice(hh * e, (hh + 1) * e)
        kaug_ref[0, hh, :, 0:e] = k_ref[:, sl].astype(BF16)
        kaug_ref[0, hh, :, e:2 * e] = posk_ref[...]
        vt_ref[0, hh] = jnp.concatenate([v_ref[:, sl].T, ones], axis=0).astype(BF16)


def _attn_prep(z, nb, col0, tk):
    n = z.shape[0]
    tp = n // nb
    nt = tp // tk
    e = DA_HEAD_DIM
    kl = jnp.arange(tk)
    hi_lo = jnp.stack([(kl // LANES) * LANES, kl % LANES], axis=1).astype(F32)
    posk = jnp.pad(hi_lo, ((0, 0), (0, e - 2))).astype(BF16)
    ck, cv = (col0 + DA_WIDTH) // DA_WIDTH, (col0 + 2 * DA_WIDTH) // DA_WIDTH
    return pl.pallas_call(
        _attn_prep_kernel,
        grid=(nb, nt),
        in_specs=[pl.BlockSpec((tk, DA_WIDTH), lambda b, t: (b * nt + t, ck)),
                  pl.BlockSpec((tk, DA_WIDTH), lambda b, t: (b * nt + t, cv)),
                  pl.BlockSpec((tk, e), lambda b, t: (0, 0))],
        out_specs=[pl.BlockSpec((1, DA_HEADS, tk, 2 * e), lambda b, t: (b, 0, t, 0)),
                   pl.BlockSpec((1, DA_HEADS, e + SUBLANES, tk), lambda b, t: (b, 0, 0, t))],
        out_shape=[jax.ShapeDtypeStruct((nb, DA_HEADS, tp, 2 * e), BF16),
                   jax.ShapeDtypeStruct((nb, DA_HEADS, e + SUBLANES, tp), BF16)],
        compiler_params=_cparams(("parallel", "parallel")),
        name="attn_prep",
    )(z, z, posk)


def _attn_kernel(qi_tab, ki_tab, slopes_ref, q_ref, kaug_ref, vt_ref, lq1_ref, lk1_ref, lq2_ref,
                 lk2_ref, g_ref, o_ref, qaug_ref, m_ref, acc_ref, *, lambda_init):
    tq = q_ref.shape[0]
    tk = kaug_ref.shape[0]
    e = DA_HEAD_DIM
    ncb = 2 * tq // ATT_CB
    h = pl.program_id(1)
    p = pl.program_id(2)
    qi = qi_tab[p]
    ki = ki_tab[p]
    slope = slopes_ref[h]

    @pl.when(ki == 0)
    def _():
        q = q_ref[...] * (DA_QK_DIM ** -0.5)
        lane = lax.broadcasted_iota(jnp.int32, (tq, e), 1)
        aug = jnp.where(lane < 2, slope, 0.0)
        rows = [jnp.concatenate([jnp.where(sel, q, 0.0), aug], axis=1)
                for sel in (lane < DA_QK_DIM, lane >= DA_QK_DIM)]
        qboth = jnp.concatenate(rows, axis=0).astype(BF16)
        for cb in range(ncb):
            qaug_ref[cb] = qboth[cb * ATT_CB:(cb + 1) * ATT_CB]
        m_ref[...] = jnp.full_like(m_ref, NEG_BIG)
        acc_ref[...] = jnp.zeros_like(acc_ref)

    off = slope * jnp.full((1, ATT_CB), (ki - qi) * tk, jnp.int32).astype(F32)

    def run(masked):
        kaug = kaug_ref[...]
        v_t = vt_ref[...]

        ahead = 1
        pending = [_dot_nt(kaug, qaug_ref[cb]) for cb in range(ahead)]
        for cb in range(ncb):
            s = pending.pop(0)
            if cb + ahead < ncb:
                pending.append(_dot_nt(kaug, qaug_ref[cb + ahead]))
            if masked:
                col = cb * ATT_CB + lax.broadcasted_iota(jnp.int32, (1, ATT_CB), 1)
                ql = jnp.where(col >= tq, col - tq, col)
                kl = lax.broadcasted_iota(jnp.int32, (tk, 1), 0)
                s = jnp.where(kl <= ql, s, NEG_BIG)
            m_prev = m_ref[cb]
            m_new = jnp.maximum(m_prev, jnp.max(s, axis=0, keepdims=True) + off)
            alpha = jnp.exp(m_prev - m_new)
            pm = jnp.exp(s - (m_new - off)).astype(BF16)
            acc_ref[cb] = alpha * acc_ref[cb] + _dot(v_t, pm)
            m_ref[cb] = m_new

    @pl.when(ki != qi)
    def _():
        run(False)

    @pl.when(ki == qi)
    def _():
        run(True)
        lam = (jnp.exp(jnp.sum(lq1_ref[...] * lk1_ref[...], axis=-1, keepdims=True))
               - jnp.exp(jnp.sum(lq2_ref[...] * lk2_ref[...], axis=-1, keepdims=True)) + lambda_init)
        acc = jnp.concatenate([acc_ref[cb, 0:e] for cb in range(ncb)], axis=1)
        den = jnp.concatenate([acc_ref[cb, e:e + 1] for cb in range(ncb)], axis=1)
        o = acc / den
        o = o[:, :tq] - lam * o[:, tq:]
        ms = jnp.mean(o * o, axis=0, keepdims=True)
        o = o * lax.rsqrt(ms + 1e-5) * (g_ref[...] * (1.0 - lambda_init))
        o_ref[...] = o.T.astype(BF16)


def _diff_attention(z, nb, lq1, lk1, lq2, lk2, subln_g, lambda_init, col0, tq=ROW_TILE):
    n = z.shape[0]
    nq = n // nb // tq
    pairs = [(qi, ki) for qi in range(nq) for ki in range(qi + 1)]
    qi_tab = jnp.asarray([p[0] for p in pairs], jnp.int32)
    ki_tab = jnp.asarray([p[1] for p in pairs], jnp.int32)
    slopes = 2.0 ** (-8.0 * jnp.arange(1, DA_HEADS + 1, dtype=F32) / DA_HEADS)
    e = DA_HEAD_DIM
    assert (2 * tq) % ATT_CB == 0 and tq % LANES == 0
    kaug, vt = _attn_prep(z, nb, col0, tq)
    cq = col0 // e
    const = lambda shape: pl.BlockSpec(shape, lambda b, h, p, qt, kt: (0, 0))
    grid_spec = pltpu.PrefetchScalarGridSpec(
        num_scalar_prefetch=2,
        grid=(nb, DA_HEADS, len(pairs)),
        in_specs=[
            pl.BlockSpec(memory_space=pltpu.SMEM),
            pl.BlockSpec((tq, e), lambda b, h, p, qt, kt: (b * nq + qt[p], cq + h)),
            pl.BlockSpec((None, None, tq, 2 * e), lambda b, h, p, qt, kt: (b, h, kt[p], 0)),
            pl.BlockSpec((None, None, e + SUBLANES, tq), lambda b, h, p, qt, kt: (b, h, 0, kt[p])),
            const((1, DA_QK_DIM)), const((1, DA_QK_DIM)), const((1, DA_QK_DIM)), const((1, DA_QK_DIM)),
            const((e, 1)),
        ],
        out_specs=pl.BlockSpec((tq, e), lambda b, h, p, qt, kt: (b * nq + qt[p], h)),
        scratch_shapes=[
            pltpu.VMEM((2 * tq // ATT_CB, ATT_CB, 2 * e), BF16),
            pltpu.VMEM((2 * tq // ATT_CB, 1, ATT_CB), F32),
            pltpu.VMEM((2 * tq // ATT_CB, e + SUBLANES, ATT_CB), F32),
        ],
    )
    r64 = lambda a: a.reshape(1, DA_QK_DIM)
    return pl.pallas_call(
        functools.partial(_attn_kernel, lambda_init=lambda_init),
        grid_spec=grid_spec,
        out_shape=jax.ShapeDtypeStruct((n, DA_WIDTH), BF16),
        compiler_params=_cparams(("parallel", "parallel", "arbitrary")),
        name="diff_attention",
    )(qi_tab, ki_tab, slopes, z, kaug, vt, r64(lq1), r64(lk1), r64(lq2), r64(lk2), subln_g.reshape(e, 1))


def _s5_disc_kernel(lr_ref, li_ref, ldt_ref, ar_ref, ai_ref, cr_ref, ci_ref):
    lr, li = lr_ref[...], li_ref[...]
    dt = jnp.exp(ldt_ref[...])
    mag = jnp.exp(lr * dt)
    ar = mag * jnp.cos(li * dt)
    ai = mag * jnp.sin(li * dt)
    den = lr * lr + li * li
    ar_ref[...] = ar
    ai_ref[...] = ai
    cr_ref[...] = ((ar - 1.0) * lr + ai * li) / den
    ci_ref[...] = (ai * lr - (ar - 1.0) * li) / den


def _s5_discretise(lam_re, lam_im, log_dt):
    shp = jax.ShapeDtypeStruct(lam_re.shape, F32)
    ldt = jnp.broadcast_to(log_dt[:, None], lam_re.shape)
    return pl.pallas_call(_s5_disc_kernel, out_shape=(shp, shp, shp, shp), name="s5_discretise")(
        lam_re, lam_im, ldt)


S5_LC = 8
S5_KB = 8
S5_KB_CH = S5_WIDTH // S5_KB
S5_KB_ST = S5_GROUPS // S5_KB * S5_STATE


def _s5_chunk_kernel(*refs):
    x_refs = refs[:S5_LC]
    t_ref, e_ref, f_ref, ar_ref, ai_ref, y_ref, g_ref, pr_ref, pi_ref = refs[S5_LC:]
    tm = y_ref.shape[1]
    ns = S5_KB_ST

    @pl.when(pl.program_id(2) == 0)
    def _():
        g_ref[0:SUBLANES, :] = jnp.zeros((SUBLANES, 2 * ns), F32)
        ar, ai = ar_ref[0], ai_ref[0]
        for _ in range(int(math.log2(S5_LC))):
            ar, ai = ar * ar - ai * ai, 2.0 * ar * ai
        pr, pi = ar, ai
        for j in range(SUBLANES):
            pr_ref[j:j + 1, :] = pr
            pi_ref[j:j + 1, :] = pi
            pr, pi = pr * ar - pi * ai, pr * ai + pi * ar

    x = jnp.concatenate([r[...] for r in x_refs], axis=1).astype(BF16)
    g_ref[SUBLANES:SUBLANES + tm, :] = _dot(x, e_ref[0])
    row = lax.broadcasted_iota(jnp.int32, (SUBLANES, ns), 0)

    def body(gi, carry):
        cr, ci = carry
        r0 = pl.multiple_of(SUBLANES + gi * SUBLANES, SUBLANES)
        xr = g_ref[pl.ds(r0, SUBLANES), 0:ns]
        xi = g_ref[pl.ds(r0, SUBLANES), ns:2 * ns]
        for s in (1, 2, 4):
            keep = row >= s
            ar = pr_ref[s - 1:s, :]
            ai = pi_ref[s - 1:s, :]
            sr = jnp.where(keep, pltpu.roll(xr, s, 0), 0.0)
            si = jnp.where(keep, pltpu.roll(xi, s, 0), 0.0)
            xr, xi = xr + ar * sr - ai * si, xi + ar * si + ai * sr
        pr, pi = pr_ref[...], pi_ref[...]
        xr, xi = xr + pr * cr - pi * ci, xi + pr * ci + pi * cr
        g_ref[pl.ds(r0, SUBLANES), 0:ns] = xr
        g_ref[pl.ds(r0, SUBLANES), ns:2 * ns] = xi
        return xr[SUBLANES - 1:SUBLANES, :], xi[SUBLANES - 1:SUBLANES, :]

    carry_in = (g_ref[SUBLANES - 1:SUBLANES, 0:ns], g_ref[SUBLANES - 1:SUBLANES, ns:2 * ns])
    lax.fori_loop(0, tm // SUBLANES, body, carry_in)
    h_start = g_ref[SUBLANES - 1:SUBLANES - 1 + tm, :].astype(BF16)
    y_ref[0] = _dot(x, t_ref[0]) + _dot(h_start, f_ref[0])
    g_ref[SUBLANES - 1:SUBLANES, :] = g_ref[SUBLANES - 1 + tm:SUBLANES + tm, :]


def _s5_chunked(u, nb, t_w, e_w, f_w, ar, ai, tm=208):
    n = u.shape[0]
    rows = n // S5_LC
    nt = rows // nb // tm
    assert nt * tm * nb == rows and tm % SUBLANES == 0
    u8 = u.reshape(rows, S5_LC * S5_WIDTH)
    kw = S5_LC * S5_KB_CH
    piece = lambda i: pl.BlockSpec((tm, S5_KB_CH), lambda kb, b, t: (b * nt + t, i * S5_KB + kb))
    wspec = lambda k, m: pl.BlockSpec((1, k, m), lambda kb, b, t: (kb, 0, 0), pipeline_mode=pl.Buffered(1))
    aspec = pl.BlockSpec((1, 1, S5_KB_ST), lambda kb, b, t: (kb, 0, 0))
    return pl.pallas_call(
        _s5_chunk_kernel,
        grid=(S5_KB, nb, nt),
        in_specs=[piece(i) for i in range(S5_LC)] + [
            wspec(kw, kw), wspec(kw, 2 * S5_KB_ST), wspec(2 * S5_KB_ST, kw), aspec, aspec],
        out_specs=pl.BlockSpec((1, tm, kw), lambda kb, b, t: (kb, b * nt + t, 0)),
        out_shape=jax.ShapeDtypeStruct((S5_KB, rows, kw), F32),
        scratch_shapes=[
            pltpu.VMEM((tm + SUBLANES, 2 * S5_KB_ST), F32),
            pltpu.VMEM((SUBLANES, S5_KB_ST), F32),
            pltpu.VMEM((SUBLANES, S5_KB_ST), F32),
        ],
        compiler_params=_cparams(("parallel", "parallel", "arbitrary")),
        name="s5_chunk",
    )(*([u8] * S5_LC), t_w, e_w, f_w, ar.reshape(S5_KB, 1, S5_KB_ST), ai.reshape(S5_KB, 1, S5_KB_ST))


def _s5_post_kernel(*refs):
    y_refs = refs[:S5_KB]
    u_ref, d_ref, gw_ref, gb_ref, o_ref = refs[S5_KB:]
    y = jnp.concatenate([r[0] for r in y_refs], axis=1)
    y = _gelu_tanh(y + d_ref[...] * u_ref[...])
    gate = jax.nn.sigmoid(_dot(y.astype(BF16), gw_ref[...]) + gb_ref[...])
    o_ref[...] = (y * gate).astype(BF16)


def _s5_post(y8, u, d_skip, glu_w, glu_b, tm=520):
    n = u.shape[0]
    rows = n // S5_LC
    w = S5_WIDTH
    u8 = u.reshape(rows, S5_LC * w)
    ypiece = lambda kb: pl.BlockSpec((1, tm, S5_KB_CH), lambda r, j: (kb, r, j))
    full = lambda shape: pl.BlockSpec(shape, lambda r, j: (0,) * len(shape))
    out = pl.pallas_call(
        _s5_post_kernel,
        grid=(rows // tm, S5_LC),
        in_specs=[ypiece(kb) for kb in range(S5_KB)] + [
            pl.BlockSpec((tm, w), lambda r, j: (r, j)), full((1, w)), full((w, w)), full((1, w))],
        out_specs=pl.BlockSpec((tm, w), lambda r, j: (r, j)),
        out_shape=jax.ShapeDtypeStruct((rows, S5_LC * w), BF16),
        compiler_params=_cparams(("parallel", "arbitrary")),
        name="s5_post",
    )(*([y8] * S5_KB), u8, d_skip.reshape(1, w), glu_w, glu_b.reshape(1, w))
    return out.reshape(n, w)


def _token_shift(x_ref, mu_ref, xe_ref, first):
    tt = x_ref.shape[0]

    @pl.when(first)
    def _():
        xe_ref[0:SUBLANES, :] = jnp.zeros((SUBLANES, x_ref.shape[1]), F32)

    x = x_ref[...]
    xe_ref[SUBLANES:SUBLANES + tt, :] = x
    xprev = xe_ref[SUBLANES - 1:SUBLANES - 1 + tt, :]
    xe_ref[0:SUBLANES, :] = x_ref[tt - SUBLANES:tt, :]
    return x + (xprev - x) * mu_ref[...]


def _rw_prep_kernel(xr_ref, xk_ref, xv_ref, xl_ref, mur_ref, muk_ref, muv_ref, mul_ref,
                    w0_ref, w2_ref, a0_ref, a2_ref, g2_ref, kk_ref, ka_ref, ones_ref,
                    r_out, lw_out, k_out, v_out, kk_out, a_out, g_out,
                    er_ref, ek_ref, ev_ref, el_ref):
    first = pl.program_id(1) == 0
    r = _token_shift(xr_ref, mur_ref, er_ref, first)
    k = _token_shift(xk_ref, muk_ref, ek_ref, first)
    v = _token_shift(xv_ref, muv_ref, ev_ref, first)
    lo = _token_shift(xl_ref, mul_ref, el_ref, first)
    wl = lo[:, 0:RW_LORA_PAD]
    al = lo[:, RW_LORA_PAD:2 * RW_LORA_PAD]
    gl = lo[:, 2 * RW_LORA_PAD:2 * RW_LORA_PAD + RW_GATE_LORA]

    wdec = -_softplus(-(w0_ref[...] + _dot(jnp.tanh(wl).astype(BF16), w2_ref[...]))) - 0.5
    a = jax.nn.sigmoid(a0_ref[...] + _dot(al.astype(BF16), a2_ref[...]))
    g = _dot(jax.nn.sigmoid(gl).astype(BF16), g2_ref[...])
    kkr = k * kk_ref[...]
    ssq = _split_dot(kkr * kkr, ones_ref[...])
    kk = kkr / jnp.maximum(jnp.sqrt(ssq), 1e-12)

    r_out[...] = r
    lw_out[...] = -jnp.exp(wdec)
    k_out[...] = k * (1.0 + (a - 1.0) * ka_ref[...])
    v_out[...] = v
    kk_out[...] = kk
    a_out[...] = a
    g_out[...] = g


RW_LORA_COLS = 2 * RW_LORA_PAD + RW_GATE_LORA


def _rw_prep(z, nb, col0, mu, w0, w2, a0, a2, g2, k_k, k_a, ones_bd, tt=320):
    n = z.shape[0]
    nt = n // nb // tt
    w = RW_WIDTH
    lw = RW_LORA_COLS
    assert col0 % w == 0 and (col0 + 3 * w) % lw == 0
    cb = col0 // w
    cl = (col0 + 3 * w) // lw
    full = lambda shape: pl.BlockSpec(shape, lambda b, t: (0,) * len(shape))
    xblk = lambda width, c: pl.BlockSpec((tt, width), lambda b, t: (b * nt + t, c))
    out = jax.ShapeDtypeStruct((n, w), F32)
    ospec = pl.BlockSpec((tt, w), lambda b, t: (b * nt + t, 0))
    mu = mu.reshape(1, 3 * w + lw)
    return pl.pallas_call(
        _rw_prep_kernel,
        grid=(nb, nt),
        in_specs=[
            xblk(w, cb), xblk(w, cb + 1), xblk(w, cb + 2), xblk(lw, cl),
            full((1, w)), full((1, w)), full((1, w)), full((1, lw)),
            full((1, w)), full((RW_LORA_PAD, w)), full((1, w)), full((RW_LORA_PAD, w)),
            full((RW_GATE_LORA, w)), full((1, w)), full((1, w)), full((w, w)),
        ],
        out_specs=[ospec] * 7,
        out_shape=[out] * 7,
        scratch_shapes=[pltpu.VMEM((tt + SUBLANES, w), F32)] * 3 + [pltpu.VMEM((tt + SUBLANES, lw), F32)],
        compiler_params=_cparams(("parallel", "arbitrary")),
        name="rwkv_prep",
    )(z, z, z, z, mu[:, 0:w], mu[:, w:2 * w], mu[:, 2 * w:3 * w], mu[:, 3 * w:],
      w0.reshape(1, w), w2, a0.reshape(1, w), a2, g2, k_k.reshape(1, w), k_a.reshape(1, w), ones_bd)


def _rw_chunk_kernel(r_ref, lw_ref, k_ref, v_ref, kk_ref, a_ref, g_ref, rk_ref, lnw_ref, lnb_ref,
                     ones_ref, o_ref, s_ref):
    @pl.when(pl.program_id(1) == 0)
    def _():
        s_ref[...] = jnp.zeros_like(s_ref)

    gw = RW_GROUP * RW_HEAD_DIM
    nb, _, width = r_ref.shape
    chains = [(b, slice(j * gw, (j + 1) * gw)) for b in range(nb) for j in range(width // gw)]
    ins = [tuple(ref[b, :, sl] for ref in (r_ref, lw_ref, k_ref, v_ref, kk_ref, a_ref))
           for b, sl in chains]
    s0s = [s_ref[i] for i in range(len(chains))]
    ys, s_news = _rw_chunk_steps(ins, s0s)
    for i, (b, sl) in enumerate(chains):
        s_ref[i] = s_news[i]
        o_ref[b, :, sl] = _rw_output(ys[i], r_ref[b, :, sl], k_ref[b, :, sl], v_ref[b, :, sl],
                                     g_ref[b, :, sl], rk_ref[:, sl], lnw_ref[:, sl], lnb_ref[:, sl],
                                     ones_ref[...])


def _rw_chunk_prepare(r, lw, k, v, kk, a):
    L = RW_CHUNK
    gw = RW_GROUP * RW_HEAD_DIM

    ti = lax.broadcasted_iota(jnp.int32, (L, L), 0)
    tj = lax.broadcasted_iota(jnp.int32, (L, L), 1)
    tri = jnp.where(ti >= tj, 1.0, 0.0).astype(BF16)
    x1 = lw.astype(BF16)
    r1 = lw - x1.astype(F32)
    x2 = r1.astype(BF16)
    x3 = (r1 - x2.astype(F32)).astype(BF16)
    cl = _dot(tri, x1) + _dot(tri, x2) + _dot(tri, x3)
    g_last = jnp.exp(cl[L - 1:L, :])
    e_in = jnp.exp(cl)
    e_ex = jnp.exp(cl - lw)
    e_inv = jnp.exp(-cl)
    beta = kk * a
    a_t = -kk * e_ex
    r_t = r * e_in
    b_t = beta * e_inv
    k_t = k * e_inv

    lane_head = lax.shift_right_logical(lax.broadcasted_iota(jnp.int32, (L, gw), 1),
                                        int(math.log2(RW_HEAD_DIM)))

    def stack(x):
        return jnp.concatenate([jnp.where(lane_head == hh, x, 0.0) for hh in range(RW_GROUP)],
                               axis=0).astype(BF16)

    ar_s = jnp.concatenate([stack(a_t), stack(r_t)], axis=0)
    bk_s = jnp.concatenate([stack(b_t), stack(k_t)], axis=0)
    return dict(ar_s=ar_s, bk_s=bk_s, v_s=stack(v), bh_s=stack(b_t * g_last), kh_s=stack(k_t * g_last),
                g_last=g_last)


def _rw_chunk_steps(ins, s0s):
    L = RW_CHUNK
    rows = RW_GROUP * L
    n = range(len(ins))
    pre = [_rw_chunk_prepare(*x) for x in ins]
    prod = [_dot_nt(p["ar_s"], p["bk_s"]) for p in pre]
    w0 = [_dot_nt(p["ar_s"], s0.astype(BF16)) for p, s0 in zip(pre, s0s)]
    ri = lax.broadcasted_iota(jnp.int32, (rows, rows), 0)
    ci = lax.broadcasted_iota(jnp.int32, (rows, rows), 1)
    strict = ri > ci
    incl = ri >= ci
    npow = [jnp.where(strict, pr[:rows, :rows], 0.0).astype(BF16) for pr in prod]
    a_ak = [jnp.where(strict, pr[:rows, rows:], 0.0).astype(BF16) for pr in prod]
    m_rb = [jnp.where(incl, pr[rows:, :rows], 0.0).astype(BF16) for pr in prod]
    m_rk = [jnp.where(incl, pr[rows:, rows:], 0.0).astype(BF16) for pr in prod]
    x = [w0[i][:rows] + _dot(a_ak[i], pre[i]["v_s"]) for i in n]
    steps = int(math.log2(L))
    for kx in range(steps):
        x = [x[i] + _dot(npow[i], x[i].astype(BF16)) for i in n]
        if kx + 1 < steps:
            npow = [_dot(npow[i], npow[i]).astype(BF16) for i in n]
    u_s = [xi.astype(BF16) for xi in x]
    y_s = [w0[i][rows:] + _dot(m_rb[i], u_s[i]) + _dot(m_rk[i], pre[i]["v_s"]) for i in n]
    ys = []
    for ysi in y_s:
        y = ysi[0:L]
        for hh in range(1, RW_GROUP):
            y = y + ysi[hh * L:(hh + 1) * L]
        ys.append(y)
    s_new = [s0s[i] * pre[i]["g_last"] + _dot_tn(u_s[i], pre[i]["bh_s"])
             + _dot_tn(pre[i]["v_s"], pre[i]["kh_s"]) for i in n]
    return ys, s_new


def _rw_output(y, r, k, v, g, r_k, ln_w, ln_b, ones):
    inv_n = 1.0 / RW_HEAD_DIM
    mean = _split_dot(y, ones) * inv_n
    yc = y - mean
    var = _split_dot(yc * yc, ones) * inv_n
    yn = yc * lax.rsqrt(var + RW_GN_EPS) * ln_w + ln_b
    bonus = _split_dot(r * k * r_k, ones) * v
    return ((yn + bonus) * g).astype(BF16)


def _rw_chunk(prep, nb, r_k, ln_w, ln_b, ones_g):
    n = prep[0].shape[0]
    L = RW_CHUNK
    gw = RW_GROUP * RW_HEAD_DIM
    bw = RW_STEP_GROUPS * gw
    tp = n // nb
    w = RW_WIDTH
    blk = lambda: pl.BlockSpec((nb, L, bw), lambda gi, c: (0, c, gi))
    par = lambda: pl.BlockSpec((1, bw), lambda gi, c: (0, gi))
    out = pl.pallas_call(
        _rw_chunk_kernel,
        grid=(w // bw, tp // L),
        in_specs=[blk() for _ in range(7)] + [par(), par(), par(),
                                             pl.BlockSpec((gw, gw), lambda gi, c: (0, 0))],
        out_specs=blk(),
        out_shape=jax.ShapeDtypeStruct((nb, tp, w), BF16),
        scratch_shapes=[pltpu.VMEM((nb * RW_STEP_GROUPS, gw, gw), F32)],
        compiler_params=_cparams(("parallel", "arbitrary")),
        name="rwkv_chunk",
    )(*[x.reshape(nb, tp, w) for x in prep], r_k.reshape(1, w), ln_w.reshape(1, w), ln_b.reshape(1, w),
      ones_g)
    return out.reshape(n, w)


def _head_ones(width, head):
    idx = jnp.arange(width) // head
    return (idx[:, None] == idx[None, :]).astype(BF16)


def _pack_lru_gates(wa, wx):
    return jnp.concatenate([wa, wx], axis=-1).astype(BF16)


def _pack_odd_in(w_in, mu):
    s5 = S5_WIDTH
    c_rkv = s5 + 3 * RW_WIDTH
    c_wl = c_rkv + RW_DECAY_LORA
    c_al = c_wl + RW_AAA_LORA
    padw = lambda x, n: jnp.pad(x, ((0, 0), (0, n - x.shape[1])))
    w = jnp.concatenate([
        w_in[:, :c_rkv],
        padw(w_in[:, c_rkv:c_wl], RW_LORA_PAD),
        padw(w_in[:, c_wl:c_al], RW_LORA_PAD),
        w_in[:, c_al:],
    ], axis=1)
    m = mu[None, :]
    o = s5
    mu_p = jnp.concatenate([
        m[:, :c_rkv - o],
        padw(m[:, c_rkv - o:c_wl - o], RW_LORA_PAD),
        padw(m[:, c_wl - o:c_al - o], RW_LORA_PAD),
        m[:, c_al - o:],
    ], axis=1)[0]
    return w.astype(BF16), mu_p


def _pad_rows(x, n):
    return jnp.pad(x, ((0, n - x.shape[0]), (0, 0)))


def _pack_s5(ar, ai, cr, ci, b_re, b_im, c_re, c_im):
    lc, nk, gpb = S5_LC, S5_KB, S5_GROUPS // S5_KB
    hp = lax.Precision.HIGHEST
    bbr = cr[..., None] * b_re - ci[..., None] * b_im
    bbi = cr[..., None] * b_im + ci[..., None] * b_re
    pr, pi = [jnp.ones_like(ar)], [jnp.zeros_like(ai)]
    for _ in range(lc):
        pr, pi = pr + [pr[-1] * ar - pi[-1] * ai], pi + [pr[-1] * ai + pi[-1] * ar]
    pr, pi = jnp.stack(pr), jnp.stack(pi)
    car = c_re[None] * pr[:, :, None, :] - c_im[None] * pi[:, :, None, :]
    cai = c_re[None] * pi[:, :, None, :] + c_im[None] * pr[:, :, None, :]
    kt = (jnp.einsum("tgcp,gpd->tgcd", car[:lc], bbr, precision=hp)
          - jnp.einsum("tgcp,gpd->tgcd", cai[:lc], bbi, precision=hp))
    kt = jnp.concatenate([kt, jnp.zeros_like(kt[:1])], axis=0)
    pos = jnp.arange(lc)
    lag = pos[None, :] - pos[:, None]
    tij = kt[jnp.where(lag >= 0, lag, lc)]
    eye = jnp.eye(gpb, dtype=F32)
    kw = lc * S5_KB_CH
    tij = tij.reshape(lc, lc, nk, gpb, S5_GROUP, S5_GROUP)
    t_w = jnp.einsum("ijkgcd,gh->kigdjhc", tij, eye).reshape(nk, kw, kw)
    rev = lc - 1 - pos
    er = pr[rev][..., None] * bbr[None] - pi[rev][..., None] * bbi[None]
    ei = pr[rev][..., None] * bbi[None] + pi[rev][..., None] * bbr[None]

    def e_part(x):
        x = x.reshape(lc, nk, gpb, S5_STATE, S5_GROUP)
        return jnp.einsum("ikgpc,gh->kigchp", x, eye).reshape(nk, kw, S5_KB_ST)

    def f_part(x):
        x = x.reshape(lc, nk, gpb, S5_GROUP, S5_STATE)
        return jnp.einsum("jkgcp,gh->kgpjhc", x, eye).reshape(nk, S5_KB_ST, kw)

    e_w = jnp.concatenate([e_part(er), e_part(ei)], axis=2)
    f_w = jnp.concatenate([f_part(car[1:]), f_part(-cai[1:])], axis=1)
    return t_w.astype(BF16), e_w.astype(BF16), f_w.astype(BF16)


def kernel(x, meta_tokens, norm_mix_g, norm_ffn_g, final_norm_g, ev_w_in, ev_conv_w, ev_conv_b, ev_lru_wa, ev_lru_ba, ev_lru_wx, ev_lru_bx, ev_lru_lambda, ev_lq1, ev_lk1, ev_lq2, ev_lk2, ev_subln_g, ev_w_out, od_w_in, od_s5_lam_re, od_s5_lam_im, od_s5_log_dt, od_s5_b_re, od_s5_b_im, od_s5_c_re, od_s5_c_im, od_s5_d, od_glu_w, od_glu_b, od_rw_mu, od_rw_w0, od_rw_w2, od_rw_a0, od_rw_a2, od_rw_g2, od_rw_kk, od_rw_ka, od_rw_rk, od_rw_ln_w, od_rw_ln_b, od_w_out, ffn_w_gate, ffn_w_up, ffn_w_down):
    nb, seq, d = x.shape
    depth = norm_mix_g.shape[0]
    t_real = N_META + seq
    tp = -(-t_real // ROW_TILE) * ROW_TILE
    meta = jnp.broadcast_to(meta_tokens[None].astype(x.dtype), (nb, N_META, d))
    pad = jnp.zeros((nb, tp - t_real, d), x.dtype)
    h = jnp.concatenate([meta, x, pad], axis=1).reshape(nb * tp, d)

    ones_rw = _head_ones(RW_WIDTH, RW_HEAD_DIM)
    ones_grp = _head_ones(RW_GROUP * RW_HEAD_DIM, RW_HEAD_DIM)

    for layer in range(depth):
        j = layer // 2
        if layer % 2 == 0:
            lambda_init = 0.8 - 0.6 * math.exp(-0.3 * layer)
            z = _rms_matmul(h, norm_mix_g[layer], ev_w_in[j].astype(BF16), tn=1280)
            ya = _lru_mixer(z, nb, ev_conv_w[j], ev_conv_b[j], _pack_lru_gates(ev_lru_wa[j], ev_lru_wx[j]),
                            ev_lru_ba[j], ev_lru_bx[j], ev_lru_lambda[j])
            yb = _diff_attention(z, nb, ev_lq1[j], ev_lk1[j], ev_lq2[j], ev_lk2[j], ev_subln_g[j],
                                 lambda_init, col0=2 * LRU_WIDTH)
            w_out = ev_w_out[j].astype(BF16)
            h = _out_proj(h, ya, yb, w_out[:LRU_WIDTH], w_out[LRU_WIDTH:])
        else:
            w_in, mu_p = _pack_odd_in(od_w_in[j], od_rw_mu[j])
            u = _rms_matmul(h, norm_mix_g[layer], w_in[:, :S5_WIDTH], tn=S5_WIDTH)
            z = _rms_matmul(h, norm_mix_g[layer], w_in[:, S5_WIDTH:], tn=(w_in.shape[1] - S5_WIDTH) // 2)
            ar, ai, cr, ci = _s5_discretise(od_s5_lam_re[j], od_s5_lam_im[j], od_s5_log_dt[j])
            t_w, e_w, f_w = _pack_s5(ar, ai, cr, ci, od_s5_b_re[j], od_s5_b_im[j], od_s5_c_re[j],
                                     od_s5_c_im[j])
            y8 = _s5_chunked(u, nb, t_w, e_w, f_w, ar, ai)
            yc = _s5_post(y8, u, od_s5_d[j], od_glu_w[j].astype(BF16), od_glu_b[j])
            prep = _rw_prep(z, nb, 0, mu_p, od_rw_w0[j],
                            _pad_rows(od_rw_w2[j], RW_LORA_PAD).astype(BF16), od_rw_a0[j],
                            _pad_rows(od_rw_a2[j], RW_LORA_PAD).astype(BF16), od_rw_g2[j].astype(BF16),
                            od_rw_kk[j], od_rw_ka[j], ones_rw)
            yd = _rw_chunk(prep, nb, od_rw_rk[j], od_rw_ln_w[j], od_rw_ln_b[j], ones_grp)
            w_out = od_w_out[j].astype(BF16)
            h = _out_proj(h, yc, yd, w_out[:S5_WIDTH], w_out[S5_WIDTH:])
        h = _ffn(h, norm_ffn_g[layer], ffn_w_gate[layer].astype(BF16), ffn_w_up[layer].astype(BF16),
                 ffn_w_down[layer].astype(BF16))
    return _final_norm(h, final_norm_g, nb, seq)
```

```python
import functools
import math

import jax
import jax.numpy as jnp
from jax import lax
from jax.experimental import pallas as pl
from jax.experimental.pallas import tpu as pltpu

F32 = jnp.float32
BF16 = jnp.bfloat16

D_MODEL = 2048
N_META = 16
NORM_EPS = 1e-6
LRU_WIDTH = 1024
LRU_BLOCKS = 8
LRU_BLOCK = 128
CONV_WIDTH = 4
LRU_C = 8.0
DA_WIDTH = 1024
DA_HEADS = 8
DA_HEAD_DIM = 128
DA_QK_DIM = 64
S5_WIDTH = 1024
S5_GROUP = 16
S5_GROUPS = 64
S5_STATE = 64
RW_WIDTH = 1024
RW_HEAD_DIM = 64
RW_HEADS = 16
RW_GN_EPS = 64e-5
RW_DECAY_LORA = 96
RW_AAA_LORA = 96
RW_GATE_LORA = 256
RW_LORA_PAD = 128
FFN_HIDDEN = 5632

SUBLANES = 8
LANES = 128
VMEM_LIMIT_BYTES = 56 * 1024 * 1024

ROW_TILE = 640
RW_CHUNK = 64
RW_GROUP = 4
RW_STEP_GROUPS = 2
NEG_BIG = -1e30


def _cparams(sem, flags=None):
    return pltpu.CompilerParams(dimension_semantics=sem, vmem_limit_bytes=VMEM_LIMIT_BYTES, flags=flags)


def _dot(a, b):
    return jnp.dot(a, b, preferred_element_type=F32)


def _dot_nt(a, b):
    return lax.dot_general(a, b, (((1,), (1,)), ((), ())), preferred_element_type=F32)


def _dot_tn(a, b):
    return lax.dot_general(a, b, (((0,), (0,)), ((), ())), preferred_element_type=F32)


def _split_dot(x, w_bf16):
    hi = x.astype(BF16)
    lo = (x - hi.astype(F32)).astype(BF16)
    return _dot(hi, w_bf16) + _dot(lo, w_bf16)


def _softplus(x):
    return jnp.maximum(x, 0.0) + jnp.log1p(jnp.exp(-jnp.abs(x)))


def _gelu_tanh(x):
    return x * (0.5 * (1.0 + jnp.tanh(0.7978845608028654 * (x + 0.044715 * (x * x * x)))))


def _rms_rows(x, g, eps):
    ms = jnp.mean(x * x, axis=-1, keepdims=True)
    return x * lax.rsqrt(ms + eps) * g


def _rms_matmul_kernel(h_ref, g_ref, w_ref, o_ref, hn_ref):
    @pl.when(pl.program_id(1) == 0)
    def _():
        hn_ref[...] = _rms_rows(h_ref[...], g_ref[...], NORM_EPS).astype(BF16)

    o_ref[...] = _dot(hn_ref[...], w_ref[...])


def _rms_matmul(h, g, w, tn=512):
    n, d = h.shape
    n_out = w.shape[1]
    tm = ROW_TILE
    return pl.pallas_call(
        _rms_matmul_kernel,
        grid=(n // tm, n_out // tn),
        in_specs=[
            pl.BlockSpec((tm, d), lambda i, j: (i, 0)),
            pl.BlockSpec((1, d), lambda i, j: (0, 0)),
            pl.BlockSpec((d, tn), lambda i, j: (0, j)),
        ],
        out_specs=pl.BlockSpec((tm, tn), lambda i, j: (i, j)),
        out_shape=jax.ShapeDtypeStruct((n, n_out), F32),
        scratch_shapes=[pltpu.VMEM((tm, d), BF16)],
        compiler_params=_cparams(("parallel", "arbitrary")),
        name="rms_matmul",
    )(h, g.reshape(1, d), w)


def _out_proj_kernel(h_ref, ya_ref, yb_ref, wa_ref, wb_ref, o_ref):
    o_ref[...] = (h_ref[...] + _dot(ya_ref[...].astype(BF16), wa_ref[...])
                  + _dot(yb_ref[...].astype(BF16), wb_ref[...]))


def _out_proj(h, ya, yb, wa, wb, tn=1024):
    n, d = h.shape
    ka, kb = ya.shape[1], yb.shape[1]
    tm = ROW_TILE
    return pl.pallas_call(
        _out_proj_kernel,
        grid=(n // tm, d // tn),
        in_specs=[
            pl.BlockSpec((tm, tn), lambda i, j: (i, j)),
            pl.BlockSpec((tm, ka), lambda i, j: (i, 0)),
            pl.BlockSpec((tm, kb), lambda i, j: (i, 0)),
            pl.BlockSpec((ka, tn), lambda i, j: (0, j)),
            pl.BlockSpec((kb, tn), lambda i, j: (0, j)),
        ],
        out_specs=pl.BlockSpec((tm, tn), lambda i, j: (i, j)),
        out_shape=jax.ShapeDtypeStruct((n, d), F32),
        compiler_params=_cparams(("parallel", "arbitrary")),
        name="out_proj",
    )(h, ya, yb, wa, wb)


def _ffn_kernel(h_ref, g_ref, wg_ref, wu_ref, wd_ref, o_ref, hn_ref, acc_ref):
    j = pl.program_id(1)

    @pl.when(j == 0)
    def _():
        hn_ref[...] = _rms_rows(h_ref[...], g_ref[...], NORM_EPS).astype(BF16)
        acc_ref[...] = jnp.zeros_like(acc_ref)

    hn = hn_ref[...]
    gate = _dot(hn, wg_ref[...])
    up = _dot(hn, wu_ref[...])
    act = (gate * jax.nn.sigmoid(gate) * up).astype(BF16)
    acc_ref[...] += _dot(act, wd_ref[...])

    @pl.when(j == pl.num_programs(1) - 1)
    def _():
        o_ref[...] = h_ref[...] + acc_ref[...]


def _ffn(h, g, wg, wu, wd, th=512):
    n, d = h.shape
    hid = wg.shape[1]
    tm = ROW_TILE
    return pl.pallas_call(
        _ffn_kernel,
        grid=(n // tm, hid // th),
        in_specs=[
            pl.BlockSpec((tm, d), lambda i, j: (i, 0)),
            pl.BlockSpec((1, d), lambda i, j: (0, 0)),
            pl.BlockSpec((d, th), lambda i, j: (0, j)),
            pl.BlockSpec((d, th), lambda i, j: (0, j)),
            pl.BlockSpec((th, d), lambda i, j: (j, 0)),
        ],
        out_specs=pl.BlockSpec((tm, d), lambda i, j: (i, 0)),
        out_shape=jax.ShapeDtypeStruct((n, d), F32),
        scratch_shapes=[pltpu.VMEM((tm, d), BF16), pltpu.VMEM((tm, d), F32)],
        compiler_params=_cparams(("parallel", "arbitrary")),
        name="ffn",
    )(h, g.reshape(1, d), wg, wu, wd)


def _final_norm_kernel(h_ref, g_ref, o_ref):
    o_ref[...] = _rms_rows(h_ref[...], g_ref[...], NORM_EPS)


def _final_norm(h, g, nb, seq, tm=512):
    n, d = h.shape
    tp = n // nb
    nt = seq // tm
    assert nt * tm == seq and N_META % SUBLANES == 0
    out = pl.pallas_call(
        _final_norm_kernel,
        grid=(nb, nt),
        in_specs=[pl.BlockSpec((pl.Element(tm), pl.Element(d)),
                               lambda b, i: (pl.multiple_of(b * tp + N_META + i * tm, SUBLANES), 0)),
                  pl.BlockSpec((1, d), lambda b, i: (0, 0))],
        out_specs=pl.BlockSpec((tm, d), lambda b, i: (b * nt + i, 0)),
        out_shape=jax.ShapeDtypeStruct((nb * seq, d), F32),
        compiler_params=_cparams(("parallel", "parallel")),
        name="final_norm",
    )(h, g.reshape(1, d))
    return out.reshape(nb, seq, d)


def _lru_kernel(xa_ref, ga_ref, cw_ref, cb_ref, wg_ref, ba_ref, bx_ref, lam_ref, o_ref,
                xe_ref, a_ref, b_ref, hc_ref):
    tt = xa_ref.shape[0]
    t = pl.program_id(1)

    @pl.when(t == 0)
    def _():
        xe_ref[0:SUBLANES, :] = jnp.zeros((SUBLANES, LRU_WIDTH), F32)
        hc_ref[...] = jnp.zeros_like(hc_ref)

    xe_ref[SUBLANES:SUBLANES + tt, :] = xa_ref[...]
    u = cb_ref[...]
    for j in range(CONV_WIDTH):
        off = SUBLANES - (CONV_WIDTH - 1) + j
        u = u + xe_ref[off:off + tt, :] * cw_ref[j:j + 1, :]
    xe_ref[0:SUBLANES, :] = xa_ref[tt - SUBLANES:tt, :]

    sp = _softplus(-lam_ref[...])
    for n in range(LRU_BLOCKS):
        sl = slice(n * LRU_BLOCK, (n + 1) * LRU_BLOCK)
        un = u[:, sl]
        zz = _dot(un.astype(BF16), wg_ref[n])
        r = jax.nn.sigmoid(zz[:, :LRU_BLOCK] + ba_ref[:, sl])
        i = jax.nn.sigmoid(zz[:, LRU_BLOCK:] + bx_ref[:, sl])
        log_a = (-LRU_C) * r * sp[:, sl]
        a = jnp.exp(log_a)
        gain = jnp.sqrt(-jnp.tanh(log_a) * (a * a + 1.0))
        a_ref[:, sl] = a
        b_ref[:, sl] = gain * (i * un)

    row = lax.broadcasted_iota(jnp.int32, (SUBLANES, LRU_WIDTH), 0)

    def body(gi, carry):
        r0 = pl.multiple_of(gi * SUBLANES, SUBLANES)
        a = a_ref[pl.ds(r0, SUBLANES), :]
        b = b_ref[pl.ds(r0, SUBLANES), :]
        for s in (1, 2, 4):
            keep = row >= s
            a_sh = jnp.where(keep, pltpu.roll(a, s, 0), 1.0)
            b_sh = jnp.where(keep, pltpu.roll(b, s, 0), 0.0)
            b = b + a * b_sh
            a = a * a_sh
        hblk = a * carry + b
        b_ref[pl.ds(r0, SUBLANES), :] = hblk
        return hblk[SUBLANES - 1:SUBLANES, :]

    hc_ref[...] = lax.fori_loop(0, tt // SUBLANES, body, hc_ref[...])
    o_ref[...] = (b_ref[...] * _gelu_tanh(ga_ref[...])).astype(BF16)


def _lru_mixer(z, nb, cw, cb, wg, ba, bx, lam, tt=ROW_TILE):
    n = z.shape[0]
    nt = n // nb // tt
    w = LRU_WIDTH
    row = lambda c: pl.BlockSpec((1, w), lambda b, t: (0, 0))
    return pl.pallas_call(
        _lru_kernel,
        grid=(nb, nt),
        in_specs=[
            pl.BlockSpec((tt, w), lambda b, t: (b * nt + t, 0)),
            pl.BlockSpec((tt, w), lambda b, t: (b * nt + t, 1)),
            pl.BlockSpec((CONV_WIDTH, w), lambda b, t: (0, 0)),
            row(0),
            pl.BlockSpec((LRU_BLOCKS, LRU_BLOCK, 2 * LRU_BLOCK), lambda b, t: (0, 0, 0)),
            row(0), row(0), row(0),
        ],
        out_specs=pl.BlockSpec((tt, w), lambda b, t: (b * nt + t, 0)),
        out_shape=jax.ShapeDtypeStruct((n, w), BF16),
        scratch_shapes=[
            pltpu.VMEM((tt + SUBLANES, w), F32),
            pltpu.VMEM((tt, w), F32),
            pltpu.VMEM((tt, w), F32),
            pltpu.VMEM((1, w), F32),
        ],
        compiler_params=_cparams(("parallel", "arbitrary")),
        name="lru_mixer",
    )(z, z, cw, cb.reshape(1, w), wg, ba.reshape(1, w), bx.reshape(1, w), lam.reshape(1, w))


ATT_CB = 256


def _attn_prep_kernel(k_ref, v_ref, posk_ref, kaug_ref, vt_ref):
    e = DA_HEAD_DIM
    tk = k_ref.shape[0]
    ones = jnp.ones((SUBLANES, tk), F32)
    for hh in range(DA_HEADS):
        sl = slice(hh * e, (hh + 1) * e)
        kaug_ref[0, hh, :, 0:e] = k_ref[:, sl].astype(BF16)
        kaug_ref[0, hh, :, e:2 * e] = posk_ref[...]
        vt_ref[0, hh] = jnp.concatenate([v_ref[:, sl].T, ones], axis=0).astype(BF16)


def _attn_prep(z, nb, col0, tk):
    n = z.shape[0]
    tp = n // nb
    nt = tp // tk
    e = DA_HEAD_DIM
    kl = jnp.arange(tk)
    hi_lo = jnp.stack([(kl // LANES) * LANES, kl % LANES], axis=1).astype(F32)
    posk = jnp.pad(hi_lo, ((0, 0), (0, e - 2))).astype(BF16)
    ck, cv = (col0 + DA_WIDTH) // DA_WIDTH, (col0 + 2 * DA_WIDTH) // DA_WIDTH
    return pl.pallas_call(
        _attn_prep_kernel,
        grid=(nb, nt),
        in_specs=[pl.BlockSpec((tk, DA_WIDTH), lambda b, t: (b * nt + t, ck)),
                  pl.BlockSpec((tk, DA_WIDTH), lambda b, t: (b * nt + t, cv)),
                  pl.BlockSpec((tk, e), lambda b, t: (0, 0))],
        out_specs=[pl.BlockSpec((1, DA_HEADS, tk, 2 * e), lambda b, t: (b, 0, t, 0)),
                   pl.BlockSpec((1, DA_HEADS, e + SUBLANES, tk), lambda b, t: (b, 0, 0, t))],
        out_shape=[jax.ShapeDtypeStruct((nb, DA_HEADS, tp, 2 * e), BF16),
                   jax.ShapeDtypeStruct((nb, DA_HEADS, e + SUBLANES, tp), BF16)],
        compiler_params=_cparams(("parallel", "parallel")),
        name="attn_prep",
    )(z, z, posk)


def _attn_kernel(qi_tab, ki_tab, slopes_ref, q_ref, kaug_ref, vt_ref, lq1_ref, lk1_ref, lq2_ref,
                 lk2_ref, g_ref, o_ref, qaug_ref, m_ref, acc_ref, *, lambda_init):
    tq = q_ref.shape[0]
    tk = kaug_ref.shape[0]
    e = DA_HEAD_DIM
    ncb = 2 * tq // ATT_CB
    h = pl.program_id(1)
    p = pl.program_id(2)
    qi = qi_tab[p]
    ki = ki_tab[p]
    slope = slopes_ref[h]

    @pl.when(ki == 0)
    def _():
        q = q_ref[...] * (DA_QK_DIM ** -0.5)
        lane = lax.broadcasted_iota(jnp.int32, (tq, e), 1)
        aug = jnp.where(lane < 2, slope, 0.0)
        rows = [jnp.concatenate([jnp.where(sel, q, 0.0), aug], axis=1)
                for sel in (lane < DA_QK_DIM, lane >= DA_QK_DIM)]
        qboth = jnp.concatenate(rows, axis=0).astype(BF16)
        for cb in range(ncb):
            qaug_ref[cb] = qboth[cb * ATT_CB:(cb + 1) * ATT_CB]
        m_ref[...] = jnp.full_like(m_ref, NEG_BIG)
        acc_ref[...] = jnp.zeros_like(acc_ref)

    off = slope * jnp.full((1, ATT_CB), (ki - qi) * tk, jnp.int32).astype(F32)

    def run(masked):
        kaug = kaug_ref[...]
        v_t = vt_ref[...]

        ahead = 1
        pending = [_dot_nt(kaug, qaug_ref[cb]) for cb in range(ahead)]
        for cb in range(ncb):
            s = pending.pop(0)
            if cb + ahead < ncb:
                pending.append(_dot_nt(kaug, qaug_ref[cb + ahead]))
            if masked:
                col = cb * ATT_CB + lax.broadcasted_iota(jnp.int32, (1, ATT_CB), 1)
                ql = jnp.where(col >= tq, col - tq, col)
                kl = lax.broadcasted_iota(jnp.int32, (tk, 1), 0)
                s = jnp.where(kl <= ql, s, NEG_BIG)
            m_prev = m_ref[cb]
            m_new = jnp.maximum(m_prev, jnp.max(s, axis=0, keepdims=True) + off)
            alpha = jnp.exp(m_prev - m_new)
            pm = jnp.exp(s - (m_new - off)).astype(BF16)
            acc_ref[cb] = alpha * acc_ref[cb] + _dot(v_t, pm)
            m_ref[cb] = m_new

    @pl.when(ki != qi)
    def _():
        run(False)

    @pl.when(ki == qi)
    def _():
        run(True)
        lam = (jnp.exp(jnp.sum(lq1_ref[...] * lk1_ref[...], axis=-1, keepdims=True))
               - jnp.exp(jnp.sum(lq2_ref[...] * lk2_ref[...], axis=-1, keepdims=True)) + lambda_init)
        acc = jnp.concatenate([acc_ref[cb, 0:e] for cb in range(ncb)], axis=1)
        den = jnp.concatenate([acc_ref[cb, e:e + 1] for cb in range(ncb)], axis=1)
        o = acc / den
        o = o[:, :tq] - lam * o[:, tq:]
        ms = jnp.mean(o * o, axis=0, keepdims=True)
        o = o * lax.rsqrt(ms + 1e-5) * (g_ref[...] * (1.0 - lambda_init))
        o_ref[...] = o.T.astype(BF16)


def _diff_attention(z, nb, lq1, lk1, lq2, lk2, subln_g, lambda_init, col0, tq=ROW_TILE):
    n = z.shape[0]
    nq = n // nb // tq
    pairs = [(qi, ki) for qi in range(nq) for ki in range(qi + 1)]
    qi_tab = jnp.asarray([p[0] for p in pairs], jnp.int32)
    ki_tab = jnp.asarray([p[1] for p in pairs], jnp.int32)
    slopes = 2.0 ** (-8.0 * jnp.arange(1, DA_HEADS + 1, dtype=F32) / DA_HEADS)
    e = DA_HEAD_DIM
    assert (2 * tq) % ATT_CB == 0 and tq % LANES == 0
    kaug, vt = _attn_prep(z, nb, col0, tq)
    cq = col0 // e
    const = lambda shape: pl.BlockSpec(shape, lambda b, h, p, qt, kt: (0, 0))
    grid_spec = pltpu.PrefetchScalarGridSpec(
        num_scalar_prefetch=2,
        grid=(nb, DA_HEADS, len(pairs)),
        in_specs=[
            pl.BlockSpec(memory_space=pltpu.SMEM),
            pl.BlockSpec((tq, e), lambda b, h, p, qt, kt: (b * nq + qt[p], cq + h)),
            pl.BlockSpec((None, None, tq, 2 * e), lambda b, h, p, qt, kt: (b, h, kt[p], 0)),
            pl.BlockSpec((None, None, e + SUBLANES, tq), lambda b, h, p, qt, kt: (b, h, 0, kt[p])),
            const((1, DA_QK_DIM)), const((1, DA_QK_DIM)), const((1, DA_QK_DIM)), const((1, DA_QK_DIM)),
            const((e, 1)),
        ],
        out_specs=pl.BlockSpec((tq, e), lambda b, h, p, qt, kt: (b * nq + qt[p], h)),
        scratch_shapes=[
            pltpu.VMEM((2 * tq // ATT_CB, ATT_CB, 2 * e), BF16),
            pltpu.VMEM((2 * tq // ATT_CB, 1, ATT_CB), F32),
            pltpu.VMEM((2 * tq // ATT_CB, e + SUBLANES, ATT_CB), F32),
        ],
    )
    r64 = lambda a: a.reshape(1, DA_QK_DIM)
    return pl.pallas_call(
        functools.partial(_attn_kernel, lambda_init=lambda_init),
        grid_spec=grid_spec,
        out_shape=jax.ShapeDtypeStruct((n, DA_WIDTH), BF16),
        compiler_params=_cparams(("parallel", "parallel", "arbitrary")),
        name="diff_attention",
    )(qi_tab, ki_tab, slopes, z, kaug, vt, r64(lq1), r64(lk1), r64(lq2), r64(lk2), subln_g.reshape(e, 1))


def _s5_disc_kernel(lr_ref, li_ref, ldt_ref, ar_ref, ai_ref, cr_ref, ci_ref):
    lr, li = lr_ref[...], li_ref[...]
    dt = jnp.exp(ldt_ref[...])
    mag = jnp.exp(lr * dt)
    ar = mag * jnp.cos(li * dt)
    ai = mag * jnp.sin(li * dt)
    den = lr * lr + li * li
    ar_ref[...] = ar
    ai_ref[...] = ai
    cr_ref[...] = ((ar - 1.0) * lr + ai * li) / den
    ci_ref[...] = (ai * lr - (ar - 1.0) * li) / den


def _s5_discretise(lam_re, lam_im, log_dt):
    shp = jax.ShapeDtypeStruct(lam_re.shape, F32)
    ldt = jnp.broadcast_to(log_dt[:, None], lam_re.shape)
    return pl.pallas_call(_s5_disc_kernel, out_shape=(shp, shp, shp, shp), name="s5_discretise")(
        lam_re, lam_im, ldt)


S5_LC = 8
S5_KB = 8
S5_KB_CH = S5_WIDTH // S5_KB
S5_KB_ST = S5_GROUPS // S5_KB * S5_STATE


def _s5_chunk_kernel(u_ref, t_ref, e_ref, f_ref, ar_ref, ai_ref, y_ref, g_ref, pr_ref, pi_ref):
    tm = u_ref.shape[0] // S5_LC
    ns = S5_KB_ST

    @pl.when(pl.program_id(2) == 0)
    def _():
        g_ref[0:SUBLANES, :] = jnp.zeros((SUBLANES, 2 * ns), F32)
        ar, ai = ar_ref[0], ai_ref[0]
        for _ in range(int(math.log2(S5_LC))):
            ar, ai = ar * ar - ai * ai, 2.0 * ar * ai
        pr, pi = ar, ai
        for j in range(SUBLANES):
            pr_ref[j:j + 1, :] = pr
            pi_ref[j:j + 1, :] = pi
            pr, pi = pr * ar - pi * ai, pr * ai + pi * ar

    x = jnp.concatenate([u_ref[pl.ds(i, tm, stride=S5_LC), :] for i in range(S5_LC)],
                        axis=1).astype(BF16)
    g_ref[SUBLANES:SUBLANES + tm, :] = _dot(x, e_ref[0])
    row = lax.broadcasted_iota(jnp.int32, (SUBLANES, ns), 0)

    def body(gi, carry):
        cr, ci = carry
        r0 = pl.multiple_of(SUBLANES + gi * SUBLANES, SUBLANES)
        xr = g_ref[pl.ds(r0, SUBLANES), 0:ns]
        xi = g_ref[pl.ds(r0, SUBLANES), ns:2 * ns]
        for s in (1, 2, 4):
            keep = row >= s
            ar = pr_ref[s - 1:s, :]
            ai = pi_ref[s - 1:s, :]
            sr = jnp.where(keep, pltpu.roll(xr, s, 0), 0.0)
            si = jnp.where(keep, pltpu.roll(xi, s, 0), 0.0)
            xr, xi = xr + ar * sr - ai * si, xi + ar * si + ai * sr
        pr, pi = pr_ref[...], pi_ref[...]
        xr, xi = xr + pr * cr - pi * ci, xi + pr * ci + pi * cr
        g_ref[pl.ds(r0, SUBLANES), 0:ns] = xr
        g_ref[pl.ds(r0, SUBLANES), ns:2 * ns] = xi
        return xr[SUBLANES - 1:SUBLANES, :], xi[SUBLANES - 1:SUBLANES, :]

    carry_in = (g_ref[SUBLANES - 1:SUBLANES, 0:ns], g_ref[SUBLANES - 1:SUBLANES, ns:2 * ns])
    lax.fori_loop(0, tm // SUBLANES, body, carry_in)
    h_start = g_ref[SUBLANES - 1:SUBLANES - 1 + tm, :].astype(BF16)
    y = _dot(x, t_ref[0]) + _dot(h_start, f_ref[0])
    for j in range(S5_LC):
        y_ref[pl.ds(j, tm, stride=S5_LC), :] = y[:, j * S5_KB_CH:(j + 1) * S5_KB_CH]
    g_ref[SUBLANES - 1:SUBLANES, :] = g_ref[SUBLANES - 1 + tm:SUBLANES + tm, :]


def _s5_chunked(u, nb, t_w, e_w, f_w, ar, ai, tm=208):
    n = u.shape[0]
    rows = n // S5_LC
    nt = rows // nb // tm
    assert nt * tm * nb == rows and tm % SUBLANES == 0 and S5_KB_CH == LANES
    kw = S5_LC * S5_KB_CH
    tile = pl.BlockSpec((tm * S5_LC, S5_KB_CH), lambda kb, b, t: (b * nt + t, kb))
    wspec = lambda k, m: pl.BlockSpec((1, k, m), lambda kb, b, t: (kb, 0, 0), pipeline_mode=pl.Buffered(1))
    aspec = pl.BlockSpec((1, 1, S5_KB_ST), lambda kb, b, t: (kb, 0, 0))
    return pl.pallas_call(
        _s5_chunk_kernel,
        grid=(S5_KB, nb, nt),
        in_specs=[tile, wspec(kw, kw), wspec(kw, 2 * S5_KB_ST), wspec(2 * S5_KB_ST, kw), aspec, aspec],
        out_specs=tile,
        out_shape=jax.ShapeDtypeStruct((n, S5_WIDTH), F32),
        scratch_shapes=[
            pltpu.VMEM((tm + SUBLANES, 2 * S5_KB_ST), F32),
            pltpu.VMEM((SUBLANES, S5_KB_ST), F32),
            pltpu.VMEM((SUBLANES, S5_KB_ST), F32),
        ],
        compiler_params=_cparams(("parallel", "parallel", "arbitrary")),
        name="s5_chunk",
    )(u, t_w, e_w, f_w, ar.reshape(S5_KB, 1, S5_KB_ST), ai.reshape(S5_KB, 1, S5_KB_ST))


def _s5_post_kernel(y_ref, u_ref, d_ref, gw_ref, gb_ref, o_ref):
    y = _gelu_tanh(y_ref[...] + d_ref[...] * u_ref[...])
    gate = jax.nn.sigmoid(_dot(y.astype(BF16), gw_ref[...]) + gb_ref[...])
    o_ref[...] = (y * gate).astype(BF16)


def _s5_post(y, u, d_skip, glu_w, glu_b):
    n, w = u.shape
    tm = ROW_TILE
    rows = pl.BlockSpec((tm, w), lambda i: (i, 0))
    full = lambda shape: pl.BlockSpec(shape, lambda i: (0,) * len(shape))
    return pl.pallas_call(
        _s5_post_kernel,
        grid=(n // tm,),
        in_specs=[rows, rows, full((1, w)), full((w, w)), full((1, w))],
        out_specs=rows,
        out_shape=jax.ShapeDtypeStruct((n, w), BF16),
        compiler_params=_cparams(("parallel",)),
        name="s5_post",
    )(y, u, d_skip.reshape(1, w), glu_w, glu_b.reshape(1, w))


def _token_shift(x_ref, mu_ref, xe_ref, first):
    tt = x_ref.shape[0]

    @pl.when(first)
    def _():
        xe_ref[0:SUBLANES, :] = jnp.zeros((SUBLANES, x_ref.shape[1]), F32)

    x = x_ref[...]
    xe_ref[SUBLANES:SUBLANES + tt, :] = x
    xprev = xe_ref[SUBLANES - 1:SUBLANES - 1 + tt, :]
    xe_ref[0:SUBLANES, :] = x_ref[tt - SUBLANES:tt, :]
    return x + (xprev - x) * mu_ref[...]


def _rw_prep_kernel(xr_ref, xk_ref, xv_ref, xl_ref, mur_ref, muk_ref, muv_ref, mul_ref,
                    w0_ref, w2_ref, a0_ref, a2_ref, g2_ref, kk_ref, ka_ref, ones_ref,
                    r_out, lw_out, k_out, v_out, kk_out, a_out, g_out,
                    er_ref, ek_ref, ev_ref, el_ref):
    first = pl.program_id(1) == 0
    r = _token_shift(xr_ref, mur_ref, er_ref, first)
    k = _token_shift(xk_ref, muk_ref, ek_ref, first)
    v = _token_shift(xv_ref, muv_ref, ev_ref, first)
    lo = _token_shift(xl_ref, mul_ref, el_ref, first)
    wl = lo[:, 0:RW_LORA_PAD]
    al = lo[:, RW_LORA_PAD:2 * RW_LORA_PAD]
    gl = lo[:, 2 * RW_LORA_PAD:2 * RW_LORA_PAD + RW_GATE_LORA]

    wdec = -_softplus(-(w0_ref[...] + _dot(jnp.tanh(wl).astype(BF16), w2_ref[...]))) - 0.5
    a = jax.nn.sigmoid(a0_ref[...] + _dot(al.astype(BF16), a2_ref[...]))
    g = _dot(jax.nn.sigmoid(gl).astype(BF16), g2_ref[...])
    kkr = k * kk_ref[...]
    ssq = _split_dot(kkr * kkr, ones_ref[...])
    kk = kkr / jnp.maximum(jnp.sqrt(ssq), 1e-12)

    r_out[...] = r
    lw_out[...] = -jnp.exp(wdec)
    k_out[...] = k * (1.0 + (a - 1.0) * ka_ref[...])
    v_out[...] = v
    kk_out[...] = kk
    a_out[...] = a
    g_out[...] = g


RW_LORA_COLS = 2 * RW_LORA_PAD + RW_GATE_LORA


def _rw_prep(z, nb, col0, mu, w0, w2, a0, a2, g2, k_k, k_a, ones_bd, tt=320):
    n = z.shape[0]
    nt = n // nb // tt
    w = RW_WIDTH
    lw = RW_LORA_COLS
    assert col0 % w == 0 and (col0 + 3 * w) % lw == 0
    cb = col0 // w
    cl = (col0 + 3 * w) // lw
    full = lambda shape: pl.BlockSpec(shape, lambda b, t: (0,) * len(shape))
    xblk = lambda width, c: pl.BlockSpec((tt, width), lambda b, t: (b * nt + t, c))
    out = jax.ShapeDtypeStruct((n, w), F32)
    ospec = pl.BlockSpec((tt, w), lambda b, t: (b * nt + t, 0))
    mu = mu.reshape(1, 3 * w + lw)
    return pl.pallas_call(
        _rw_prep_kernel,
        grid=(nb, nt),
        in_specs=[
            xblk(w, cb), xblk(w, cb + 1), xblk(w, cb + 2), xblk(lw, cl),
            full((1, w)), full((1, w)), full((1, w)), full((1, lw)),
            full((1, w)), full((RW_LORA_PAD, w)), full((1, w)), full((RW_LORA_PAD, w)),
            full((RW_GATE_LORA, w)), full((1, w)), full((1, w)), full((w, w)),
        ],
        out_specs=[ospec] * 7,
        out_shape=[out] * 7,
        scratch_shapes=[pltpu.VMEM((tt + SUBLANES, w), F32)] * 3 + [pltpu.VMEM((tt + SUBLANES, lw), F32)],
        compiler_params=_cparams(("parallel", "arbitrary")),
        name="rwkv_prep",
    )(z, z, z, z, mu[:, 0:w], mu[:, w:2 * w], mu[:, 2 * w:3 * w], mu[:, 3 * w:],
      w0.reshape(1, w), w2, a0.reshape(1, w), a2, g2, k_k.reshape(1, w), k_a.reshape(1, w), ones_bd)


def _rw_chunk_kernel(r_ref, lw_ref, k_ref, v_ref, kk_ref, a_ref, g_ref, rk_ref, lnw_ref, lnb_ref,
                     ones_ref, o_ref, s_ref):
    @pl.when(pl.program_id(1) == 0)
    def _():
        s_ref[...] = jnp.zeros_like(s_ref)

    gw = RW_GROUP * RW_HEAD_DIM
    nb, _, width = r_ref.shape
    chains = [(b, slice(j * gw, (j + 1) * gw)) for b in range(nb) for j in range(width // gw)]
    ins = [tuple(ref[b, :, sl] for ref in (r_ref, lw_ref, k_ref, v_ref, kk_ref, a_ref))
           for b, sl in chains]
    s0s = [s_ref[i] for i in range(len(chains))]
    ys, s_news = _rw_chunk_steps(ins, s0s)
    for i, (b, sl) in enumerate(chains):
        s_ref[i] = s_news[i]
        o_ref[b, :, sl] = _rw_output(ys[i], r_ref[b, :, sl], k_ref[b, :, sl], v_ref[b, :, sl],
                                     g_ref[b, :, sl], rk_ref[:, sl], lnw_ref[:, sl], lnb_ref[:, sl],
                                     ones_ref[...])


def _rw_chunk_prepare(r, lw, k, v, kk, a):
    L = RW_CHUNK
    gw = RW_GROUP * RW_HEAD_DIM

    ti = lax.broadcasted_iota(jnp.int32, (L, L), 0)
    tj = lax.broadcasted_iota(jnp.int32, (L, L), 1)
    tri = jnp.where(ti >= tj, 1.0, 0.0).astype(BF16)
    x1 = lw.astype(BF16)
    r1 = lw - x1.astype(F32)
    x2 = r1.astype(BF16)
    x3 = (r1 - x2.astype(F32)).astype(BF16)
    cl = _dot(tri, x1) + _dot(tri, x2) + _dot(tri, x3)
    g_last = jnp.exp(cl[L - 1:L, :])
    e_in = jnp.exp(cl)
    e_ex = jnp.exp(cl - lw)
    e_inv = jnp.exp(-cl)
    beta = kk * a
    a_t = -kk * e_ex
    r_t = r * e_in
    b_t = beta * e_inv
    k_t = k * e_inv

    lane_head = lax.shift_right_logical(lax.broadcasted_iota(jnp.int32, (L, gw), 1),
                                        int(math.log2(RW_HEAD_DIM)))

    def stack(x):
        return jnp.concatenate([jnp.where(lane_head == hh, x, 0.0) for hh in range(RW_GROUP)],
                               axis=0).astype(BF16)

    ar_s = jnp.concatenate([stack(a_t), stack(r_t)], axis=0)
    bk_s = jnp.concatenate([stack(b_t), stack(k_t)], axis=0)
    return dict(ar_s=ar_s, bk_s=bk_s, v_s=stack(v), bh_s=stack(b_t * g_last), kh_s=stack(k_t * g_last),
                g_last=g_last)


def _rw_chunk_steps(ins, s0s):
    L = RW_CHUNK
    rows = RW_GROUP * L
    n = range(len(ins))
    pre = [_rw_chunk_prepare(*x) for x in ins]
    prod = [_dot_nt(p["ar_s"], p["bk_s"]) for p in pre]
    w0 = [_dot_nt(p["ar_s"], s0.astype(BF16)) for p, s0 in zip(pre, s0s)]
    ri = lax.broadcasted_iota(jnp.int32, (rows, rows), 0)
    ci = lax.broadcasted_iota(jnp.int32, (rows, rows), 1)
    strict = ri > ci
    incl = ri >= ci
    npow = [jnp.where(strict, pr[:rows, :rows], 0.0).astype(BF16) for pr in prod]
    a_ak = [jnp.where(strict, pr[:rows, rows:], 0.0).astype(BF16) for pr in prod]
    m_rb = [jnp.where(incl, pr[rows:, :rows], 0.0).astype(BF16) for pr in prod]
    m_rk = [jnp.where(incl, pr[rows:, rows:], 0.0).astype(BF16) for pr in prod]
    x = [w0[i][:rows] + _dot(a_ak[i], pre[i]["v_s"]) for i in n]
    steps = int(math.log2(L))
    for kx in range(steps):
        x = [x[i] + _dot(npow[i], x[i].astype(BF16)) for i in n]
        if kx + 1 < steps:
            npow = [_dot(npow[i], npow[i]).astype(BF16) for i in n]
    u_s = [xi.astype(BF16) for xi in x]
    y_s = [w0[i][rows:] + _dot(m_rb[i], u_s[i]) + _dot(m_rk[i], pre[i]["v_s"]) for i in n]
    ys = []
    for ysi in y_s:
        y = ysi[0:L]
        for hh in range(1, RW_GROUP):
            y = y + ysi[hh * L:(hh + 1) * L]
        ys.append(y)
    s_new = [s0s[i] * pre[i]["g_last"] + _dot_tn(u_s[i], pre[i]["bh_s"])
             + _dot_tn(pre[i]["v_s"], pre[i]["kh_s"]) for i in n]
    return ys, s_new


def _rw_output(y, r, k, v, g, r_k, ln_w, ln_b, ones):
    inv_n = 1.0 / RW_HEAD_DIM
    mean = _split_dot(y, ones) * inv_n
    yc = y - mean
    var = _split_dot(yc * yc, ones) * inv_n
    yn = yc * lax.rsqrt(var + RW_GN_EPS) * ln_w + ln_b
    bonus = _split_dot(r * k * r_k, ones) * v
    return ((yn + bonus) * g).astype(BF16)


def _rw_chunk(prep, nb, r_k, ln_w, ln_b, ones_g):
    n = prep[0].shape[0]
    L = RW_CHUNK
    gw = RW_GROUP * RW_HEAD_DIM
    bw = RW_STEP_GROUPS * gw
    tp = n // nb
    w = RW_WIDTH
    blk = lambda: pl.BlockSpec((nb, L, bw), lambda gi, c: (0, c, gi))
    par = lambda: pl.BlockSpec((1, bw), lambda gi, c: (0, gi))
    out = pl.pallas_call(
        _rw_chunk_kernel,
        grid=(w // bw, tp // L),
        in_specs=[blk() for _ in range(7)] + [par(), par(), par(),
                                             pl.BlockSpec((gw, gw), lambda gi, c: (0, 0))],
        out_specs=blk(),
        out_shape=jax.ShapeDtypeStruct((nb, tp, w), BF16),
        scratch_shapes=[pltpu.VMEM((nb * RW_STEP_GROUPS, gw, gw), F32)],
        compiler_params=_cparams(("parallel", "arbitrary")),
        name="rwkv_chunk",
    )(*[x.reshape(nb, tp, w) for x in prep], r_k.reshape(1, w), ln_w.reshape(1, w), ln_b.reshape(1, w),
      ones_g)
    return out.reshape(n, w)


def _head_ones(width, head):
    idx = jnp.arange(width) // head
    return (idx[:, None] == idx[None, :]).astype(BF16)


def _pack_lru_gates(wa, wx):
    return jnp.concatenate([wa, wx], axis=-1).astype(BF16)


def _pack_odd_in(w_in, mu):
    s5 = S5_WIDTH
    c_rkv = s5 + 3 * RW_WIDTH
    c_wl = c_rkv + RW_DECAY_LORA
    c_al = c_wl + RW_AAA_LORA
    padw = lambda x, n: jnp.pad(x, ((0, 0), (0, n - x.shape[1])))
    w = jnp.concatenate([
        w_in[:, :c_rkv],
        padw(w_in[:, c_rkv:c_wl], RW_LORA_PAD),
        padw(w_in[:, c_wl:c_al], RW_LORA_PAD),
        w_in[:, c_al:],
    ], axis=1)
    m = mu[None, :]
    o = s5
    mu_p = jnp.concatenate([
        m[:, :c_rkv - o],
        padw(m[:, c_rkv - o:c_wl - o], RW_LORA_PAD),
        padw(m[:, c_wl - o:c_al - o], RW_LORA_PAD),
        m[:, c_al - o:],
    ], axis=1)[0]
    return w.astype(BF16), mu_p


def _pad_rows(x, n):
    return jnp.pad(x, ((0, n - x.shape[0]), (0, 0)))


def _pack_s5(ar, ai, cr, ci, b_re, b_im, c_re, c_im):
    lc, nk, gpb = S5_LC, S5_KB, S5_GROUPS // S5_KB
    hp = lax.Precision.HIGHEST
    bbr = cr[..., None] * b_re - ci[..., None] * b_im
    bbi = cr[..., None] * b_im + ci[..., None] * b_re
    pr, pi = [jnp.ones_like(ar)], [jnp.zeros_like(ai)]
    for _ in range(lc):
        pr, pi = pr + [pr[-1] * ar - pi[-1] * ai], pi + [pr[-1] * ai + pi[-1] * ar]
    pr, pi = jnp.stack(pr), jnp.stack(pi)
    car = c_re[None] * pr[:, :, None, :] - c_im[None] * pi[:, :, None, :]
    cai = c_re[None] * pi[:, :, None, :] + c_im[None] * pr[:, :, None, :]
    kt = (jnp.einsum("tgcp,gpd->tgcd", car[:lc], bbr, precision=hp)
          - jnp.einsum("tgcp,gpd->tgcd", cai[:lc], bbi, precision=hp))
    kt = jnp.concatenate([kt, jnp.zeros_like(kt[:1])], axis=0)
    pos = jnp.arange(lc)
    lag = pos[None, :] - pos[:, None]
    tij = kt[jnp.where(lag >= 0, lag, lc)]
    kw = lc * S5_KB_CH

    def expand(x, perm, g_axis, h_axis):
        x = jnp.expand_dims(x.astype(BF16).transpose(perm), h_axis)
        idx = lambda axis: jnp.arange(gpb).reshape([gpb if a == axis else 1 for a in range(x.ndim)])
        return jnp.where(idx(g_axis) == idx(h_axis), x, jnp.zeros((), BF16))

    tij = tij.reshape(lc, lc, nk, gpb, S5_GROUP, S5_GROUP)
    t_w = expand(tij, (2, 0, 3, 5, 1, 4), 2, 5).reshape(nk, kw, kw)
    rev = lc - 1 - pos
    er = pr[rev][..., None] * bbr[None] - pi[rev][..., None] * bbi[None]
    ei = pr[rev][..., None] * bbi[None] + pi[rev][..., None] * bbr[None]

    def e_part(x):
        x = x.reshape(lc, nk, gpb, S5_STATE, S5_GROUP)
        return expand(x, (1, 0, 2, 4, 3), 2, 4).reshape(nk, kw, S5_KB_ST)

    def f_part(x):
        x = x.reshape(lc, nk, gpb, S5_GROUP, S5_STATE)
        return expand(x, (1, 2, 4, 0, 3), 1, 4).reshape(nk, S5_KB_ST, kw)

    e_w = jnp.concatenate([e_part(er), e_part(ei)], axis=2)
    f_w = jnp.concatenate([f_part(car[1:]), f_part(-cai[1:])], axis=1)
    return t_w, e_w, f_w


def kernel(x, meta_tokens, norm_mix_g, norm_ffn_g, final_norm_g, ev_w_in, ev_conv_w, ev_conv_b, ev_lru_wa, ev_lru_ba, ev_lru_wx, ev_lru_bx, ev_lru_lambda, ev_lq1, ev_lk1, ev_lq2, ev_lk2, ev_subln_g, ev_w_out, od_w_in, od_s5_lam_re, od_s5_lam_im, od_s5_log_dt, od_s5_b_re, od_s5_b_im, od_s5_c_re, od_s5_c_im, od_s5_d, od_glu_w, od_glu_b, od_rw_mu, od_rw_w0, od_rw_w2, od_rw_a0, od_rw_a2, od_rw_g2, od_rw_kk, od_rw_ka, od_rw_rk, od_rw_ln_w, od_rw_ln_b, od_w_out, ffn_w_gate, ffn_w_up, ffn_w_down):
    nb, seq, d = x.shape
    depth = norm_mix_g.shape[0]
    t_real = N_META + seq
    tp = -(-t_real // ROW_TILE) * ROW_TILE
    meta = jnp.broadcast_to(meta_tokens[None].astype(x.dtype), (nb, N_META, d))
    pad = jnp.zeros((nb, tp - t_real, d), x.dtype)
    h = jnp.concatenate([meta, x, pad], axis=1).reshape(nb * tp, d)

    ones_rw = _head_ones(RW_WIDTH, RW_HEAD_DIM)
    ones_grp = _head_ones(RW_GROUP * RW_HEAD_DIM, RW_HEAD_DIM)

    for layer in range(depth):
        j = layer // 2
        if layer % 2 == 0:
            lambda_init = 0.8 - 0.6 * math.exp(-0.3 * layer)
            z = _rms_matmul(h, norm_mix_g[layer], ev_w_in[j].astype(BF16), tn=1280)
            ya = _lru_mixer(z, nb, ev_conv_w[j], ev_conv_b[j], _pack_lru_gates(ev_lru_wa[j], ev_lru_wx[j]),
                            ev_lru_ba[j], ev_lru_bx[j], ev_lru_lambda[j])
            yb = _diff_attention(z, nb, ev_lq1[j], ev_lk1[j], ev_lq2[j], ev_lk2[j], ev_subln_g[j],
                                 lambda_init, col0=2 * LRU_WIDTH)
            w_out = ev_w_out[j].astype(BF16)
            h = _out_proj(h, ya, yb, w_out[:LRU_WIDTH], w_out[LRU_WIDTH:])
        else:
            w_in, mu_p = _pack_odd_in(od_w_in[j], od_rw_mu[j])
            u = _rms_matmul(h, norm_mix_g[layer], w_in[:, :S5_WIDTH], tn=S5_WIDTH)
            z = _rms_matmul(h, norm_mix_g[layer], w_in[:, S5_WIDTH:], tn=(w_in.shape[1] - S5_WIDTH) // 2)
            ar, ai, cr, ci = _s5_discretise(od_s5_lam_re[j], od_s5_lam_im[j], od_s5_log_dt[j])
            t_w, e_w, f_w = _pack_s5(ar, ai, cr, ci, od_s5_b_re[j], od_s5_b_im[j], od_s5_c_re[j],
                                     od_s5_c_im[j])
            y8 = _s5_chunked(u, nb, t_w, e_w, f_w, ar, ai)
            yc = _s5_post(y8, u, od_s5_d[j], od_glu_w[j].astype(BF16), od_glu_b[j])
            prep = _rw_prep(z, nb, 0, mu_p, od_rw_w0[j],
                            _pad_rows(od_rw_w2[j], RW_LORA_PAD).astype(BF16), od_rw_a0[j],
                            _pad_rows(od_rw_a2[j], RW_LORA_PAD).astype(BF16), od_rw_g2[j].astype(BF16),
                            od_rw_kk[j], od_rw_ka[j], ones_rw)
            yd = _rw_chunk(prep, nb, od_rw_rk[j], od_rw_ln_w[j], od_rw_ln_b[j], ones_grp)
            w_out = od_w_out[j].astype(BF16)
            h = _out_proj(h, yc, yd, w_out[:S5_WIDTH], w_out[S5_WIDTH:])
        h = _ffn(h, norm_ffn_g[layer], ffn_w_gate[layer].astype(BF16), ffn_w_up[layer].astype(BF16),
                 ffn_w_down[layer].astype(BF16))
    return _final_norm(h, final_norm_g, nb, seq)
```

```python
import functools
import math

import jax
import jax.numpy as jnp
from jax import lax
from jax.experimental import pallas as pl
from jax.experimental.pallas import tpu as pltpu

F32 = jnp.float32
BF16 = jnp.bfloat16

D_MODEL = 2048
N_META = 16
NORM_EPS = 1e-6
LRU_WIDTH = 1024
LRU_BLOCKS = 8
LRU_BLOCK = 128
CONV_WIDTH = 4
LRU_C = 8.0
DA_WIDTH = 1024
DA_HEADS = 8
DA_HEAD_DIM = 128
DA_QK_DIM = 64
S5_WIDTH = 1024
S5_GROUP = 16
S5_GROUPS = 64
S5_STATE = 64
RW_WIDTH = 1024
RW_HEAD_DIM = 64
RW_HEADS = 16
RW_GN_EPS = 64e-5
RW_DECAY_LORA = 96
RW_AAA_LORA = 96
RW_GATE_LORA = 256
RW_LORA_PAD = 128
FFN_HIDDEN = 5632

SUBLANES = 8
LANES = 128
VMEM_LIMIT_BYTES = 56 * 1024 * 1024

ROW_TILE = 640
RW_CHUNK = 64
RW_GROUP = 4
RW_STEP_GROUPS = 2
NEG_BIG = -1e30


def _cparams(sem, flags=None):
    return pltpu.CompilerParams(dimension_semantics=sem, vmem_limit_bytes=VMEM_LIMIT_BYTES, flags=flags)


def _dot(a, b):
    return jnp.dot(a, b, preferred_element_type=F32)


def _dot_nt(a, b):
    return lax.dot_general(a, b, (((1,), (1,)), ((), ())), preferred_element_type=F32)


def _dot_tn(a, b):
    return lax.dot_general(a, b, (((0,), (0,)), ((), ())), preferred_element_type=F32)


def _split_dot(x, w_bf16):
    hi = x.astype(BF16)
    lo = (x - hi.astype(F32)).astype(BF16)
    return _dot(hi, w_bf16) + _dot(lo, w_bf16)


def _softplus(x):
    return jnp.maximum(x, 0.0) + jnp.log1p(jnp.exp(-jnp.abs(x)))


def _gelu_tanh(x):
    return x * (0.5 * (1.0 + jnp.tanh(0.7978845608028654 * (x + 0.044715 * (x * x * x)))))


def _rms_rows(x, g, eps):
    ms = jnp.mean(x * x, axis=-1, keepdims=True)
    return x * lax.rsqrt(ms + eps) * g


def _rms_matmul_kernel(h_ref, g_ref, w_ref, o_ref, hn_ref):
    @pl.when(pl.program_id(1) == 0)
    def _():
        hn_ref[...] = _rms_rows(h_ref[...], g_ref[...], NORM_EPS).astype(BF16)

    o_ref[...] = _dot(hn_ref[...], w_ref[...])


def _rms_matmul(h, g, w, tn=512):
    n, d = h.shape
    n_out = w.shape[1]
    tm = ROW_TILE
    return pl.pallas_call(
        _rms_matmul_kernel,
        grid=(n // tm, n_out // tn),
        in_specs=[
            pl.BlockSpec((tm, d), lambda i, j: (i, 0)),
            pl.BlockSpec((1, d), lambda i, j: (0, 0)),
            pl.BlockSpec((d, tn), lambda i, j: (0, j)),
        ],
        out_specs=pl.BlockSpec((tm, tn), lambda i, j: (i, j)),
        out_shape=jax.ShapeDtypeStruct((n, n_out), F32),
        scratch_shapes=[pltpu.VMEM((tm, d), BF16)],
        compiler_params=_cparams(("parallel", "arbitrary")),
        name="rms_matmul",
    )(h, g.reshape(1, d), w)


def _out_proj_kernel(h_ref, ya_ref, yb_ref, wa_ref, wb_ref, o_ref):
    o_ref[...] = (h_ref[...] + _dot(ya_ref[...].astype(BF16), wa_ref[...])
                  + _dot(yb_ref[...].astype(BF16), wb_ref[...]))


def _out_proj(h, ya, yb, wa, wb, tn=1024):
    n, d = h.shape
    ka, kb = ya.shape[1], yb.shape[1]
    tm = ROW_TILE
    return pl.pallas_call(
        _out_proj_kernel,
        grid=(n // tm, d // tn),
        in_specs=[
            pl.BlockSpec((tm, tn), lambda i, j: (i, j)),
            pl.BlockSpec((tm, ka), lambda i, j: (i, 0)),
            pl.BlockSpec((tm, kb), lambda i, j: (i, 0)),
            pl.BlockSpec((ka, tn), lambda i, j: (0, j)),
            pl.BlockSpec((kb, tn), lambda i, j: (0, j)),
        ],
        out_specs=pl.BlockSpec((tm, tn), lambda i, j: (i, j)),
        out_shape=jax.ShapeDtypeStruct((n, d), F32),
        compiler_params=_cparams(("parallel", "arbitrary")),
        name="out_proj",
    )(h, ya, yb, wa, wb)


def _ffn_kernel(h_ref, g_ref, wg_ref, wu_ref, wd_ref, o_ref, hn_ref, acc_ref):
    j = pl.program_id(1)

    @pl.when(j == 0)
    def _():
        hn_ref[...] = _rms_rows(h_ref[...], g_ref[...], NORM_EPS).astype(BF16)
        acc_ref[...] = jnp.zeros_like(acc_ref)

    hn = hn_ref[...]
    gate = _dot(hn, wg_ref[...])
    up = _dot(hn, wu_ref[...])
    act = (gate * jax.nn.sigmoid(gate) * up).astype(BF16)
    acc_ref[...] += _dot(act, wd_ref[...])

    @pl.when(j == pl.num_programs(1) - 1)
    def _():
        o_ref[...] = h_ref[...] + acc_ref[...]


def _ffn(h, g, wg, wu, wd, th=512):
    n, d = h.shape
    hid = wg.shape[1]
    tm = ROW_TILE
    return pl.pallas_call(
        _ffn_kernel,
        grid=(n // tm, hid // th),
        in_specs=[
            pl.BlockSpec((tm, d), lambda i, j: (i, 0)),
            pl.BlockSpec((1, d), lambda i, j: (0, 0)),
            pl.BlockSpec((d, th), lambda i, j: (0, j)),
            pl.BlockSpec((d, th), lambda i, j: (0, j)),
            pl.BlockSpec((th, d), lambda i, j: (j, 0)),
        ],
        out_specs=pl.BlockSpec((tm, d), lambda i, j: (i, 0)),
        out_shape=jax.ShapeDtypeStruct((n, d), F32),
        scratch_shapes=[pltpu.VMEM((tm, d), BF16), pltpu.VMEM((tm, d), F32)],
        compiler_params=_cparams(("parallel", "arbitrary")),
        name="ffn",
    )(h, g.reshape(1, d), wg, wu, wd)


def _final_norm_kernel(h_ref, g_ref, o_ref):
    o_ref[...] = _rms_rows(h_ref[...], g_ref[...], NORM_EPS)


def _final_norm(h, g, nb, seq, tm=512):
    n, d = h.shape
    tp = n // nb
    nt = seq // tm
    assert nt * tm == seq and N_META % SUBLANES == 0
    out = pl.pallas_call(
        _final_norm_kernel,
        grid=(nb, nt),
        in_specs=[pl.BlockSpec((pl.Element(tm), pl.Element(d)),
                               lambda b, i: (pl.multiple_of(b * tp + N_META + i * tm, SUBLANES), 0)),
                  pl.BlockSpec((1, d), lambda b, i: (0, 0))],
        out_specs=pl.BlockSpec((tm, d), lambda b, i: (b * nt + i, 0)),
        out_shape=jax.ShapeDtypeStruct((nb * seq, d), F32),
        compiler_params=_cparams(("parallel", "parallel")),
        name="final_norm",
    )(h, g.reshape(1, d))
    return out.reshape(nb, seq, d)


def _lru_kernel(xa_ref, ga_ref, cw_ref, cb_ref, wg_ref, ba_ref, bx_ref, lam_ref, o_ref,
                xe_ref, a_ref, b_ref, hc_ref):
    tt = xa_ref.shape[0]
    t = pl.program_id(1)

    @pl.when(t == 0)
    def _():
        xe_ref[0:SUBLANES, :] = jnp.zeros((SUBLANES, LRU_WIDTH), F32)
        hc_ref[...] = jnp.zeros_like(hc_ref)

    xe_ref[SUBLANES:SUBLANES + tt, :] = xa_ref[...]
    u = cb_ref[...]
    for j in range(CONV_WIDTH):
        off = SUBLANES - (CONV_WIDTH - 1) + j
        u = u + xe_ref[off:off + tt, :] * cw_ref[j:j + 1, :]
    xe_ref[0:SUBLANES, :] = xa_ref[tt - SUBLANES:tt, :]

    sp = _softplus(-lam_ref[...])
    for n in range(LRU_BLOCKS):
        sl = slice(n * LRU_BLOCK, (n + 1) * LRU_BLOCK)
        un = u[:, sl]
        zz = _dot(un.astype(BF16), wg_ref[n])
        r = jax.nn.sigmoid(zz[:, :LRU_BLOCK] + ba_ref[:, sl])
        i = jax.nn.sigmoid(zz[:, LRU_BLOCK:] + bx_ref[:, sl])
        log_a = (-LRU_C) * r * sp[:, sl]
        a = jnp.exp(log_a)
        gain = jnp.sqrt(-jnp.tanh(log_a) * (a * a + 1.0))
        a_ref[:, sl] = a
        b_ref[:, sl] = gain * (i * un)

    row = lax.broadcasted_iota(jnp.int32, (SUBLANES, LRU_WIDTH), 0)

    def body(gi, carry):
        r0 = pl.multiple_of(gi * SUBLANES, SUBLANES)
        a = a_ref[pl.ds(r0, SUBLANES), :]
        b = b_ref[pl.ds(r0, SUBLANES), :]
        for s in (1, 2, 4):
            keep = row >= s
            a_sh = jnp.where(keep, pltpu.roll(a, s, 0), 1.0)
            b_sh = jnp.where(keep, pltpu.roll(b, s, 0), 0.0)
            b = b + a * b_sh
            a = a * a_sh
        hblk = a * carry + b
        b_ref[pl.ds(r0, SUBLANES), :] = hblk
        return hblk[SUBLANES - 1:SUBLANES, :]

    hc_ref[...] = lax.fori_loop(0, tt // SUBLANES, body, hc_ref[...])
    o_ref[...] = (b_ref[...] * _gelu_tanh(ga_ref[...])).astype(BF16)


def _lru_mixer(z, nb, cw, cb, wg, ba, bx, lam, tt=ROW_TILE):
    n = z.shape[0]
    nt = n // nb // tt
    w = LRU_WIDTH
    row = lambda c: pl.BlockSpec((1, w), lambda b, t: (0, 0))
    return pl.pallas_call(
        _lru_kernel,
        grid=(nb, nt),
        in_specs=[
            pl.BlockSpec((tt, w), lambda b, t: (b * nt + t, 0)),
            pl.BlockSpec((tt, w), lambda b, t: (b * nt + t, 1)),
            pl.BlockSpec((CONV_WIDTH, w), lambda b, t: (0, 0)),
            row(0),
            pl.BlockSpec((LRU_BLOCKS, LRU_BLOCK, 2 * LRU_BLOCK), lambda b, t: (0, 0, 0)),
            row(0), row(0), row(0),
        ],
        out_specs=pl.BlockSpec((tt, w), lambda b, t: (b * nt + t, 0)),
        out_shape=jax.ShapeDtypeStruct((n, w), BF16),
        scratch_shapes=[
            pltpu.VMEM((tt + SUBLANES, w), F32),
            pltpu.VMEM((tt, w), F32),
            pltpu.VMEM((tt, w), F32),
            pltpu.VMEM((1, w), F32),
        ],
        compiler_params=_cparams(("parallel", "arbitrary")),
        name="lru_mixer",
    )(z, z, cw, cb.reshape(1, w), wg, ba.reshape(1, w), bx.reshape(1, w), lam.reshape(1, w))


ATT_CB = 256


def _attn_prep_kernel(k_ref, v_ref, posk_ref, kaug_ref, vt_ref):
    e = DA_HEAD_DIM
    tk = k_ref.shape[0]
    ones = jnp.ones((SUBLANES, tk), F32)
    for hh in range(DA_HEADS):
        sl = slice(hh * e, (hh + 1) * e)
        kaug_ref[0, hh, :, 0:e] = k_ref[:, sl].astype(BF16)
        kaug_ref[0, hh, :, e:2 * e] = posk_ref[...]
        vt_ref[0, hh] = jnp.concatenate([v_ref[:, sl].T, ones], axis=0).astype(BF16)


def _attn_prep(z, nb, col0, tk):
    n = z.shape[0]
    tp = n // nb
    nt = tp // tk
    e = DA_HEAD_DIM
    kl = jnp.arange(tk)
    hi_lo = jnp.stack([(kl // LANES) * LANES, kl % LANES], axis=1).astype(F32)
    posk = jnp.pad(hi_lo, ((0, 0), (0, e - 2))).astype(BF16)
    ck, cv = (col0 + DA_WIDTH) // DA_WIDTH, (col0 + 2 * DA_WIDTH) // DA_WIDTH
    return pl.pallas_call(
        _attn_prep_kernel,
        grid=(nb, nt),
        in_specs=[pl.BlockSpec((tk, DA_WIDTH), lambda b, t: (b * nt + t, ck)),
                  pl.BlockSpec((tk, DA_WIDTH), lambda b, t: (b * nt + t, cv)),
                  pl.BlockSpec((tk, e), lambda b, t: (0, 0))],
        out_specs=[pl.BlockSpec((1, DA_HEADS, tk, 2 * e), lambda b, t: (b, 0, t, 0)),
                   pl.BlockSpec((1, DA_HEADS, e + SUBLANES, tk), lambda b, t: (b, 0, 0, t))],
        out_shape=[jax.ShapeDtypeStruct((nb, DA_HEADS, tp, 2 * e), BF16),
                   jax.ShapeDtypeStruct((nb, DA_HEADS, e + SUBLANES, tp), BF16)],
        compiler_params=_cparams(("parallel", "parallel")),
        name="attn_prep",
    )(z, z, posk)


def _attn_kernel(qi_tab, ki_tab, slopes_ref, q_ref, kaug_ref, vt_ref, lq1_ref, lk1_ref, lq2_ref,
                 lk2_ref, g_ref, o_ref, qaug_ref, m_ref, acc_ref, *, lambda_init):
    tq = q_ref.shape[0]
    tk = kaug_ref.shape[0]
    e = DA_HEAD_DIM
    ncb = 2 * tq // ATT_CB
    h = pl.program_id(1)
    p = pl.program_id(2)
    qi = qi_tab[p]
    ki = ki_tab[p]
    slope = slopes_ref[h]

    @pl.when(ki == 0)
    def _():
        q = q_ref[...] * (DA_QK_DIM ** -0.5)
        lane = lax.broadcasted_iota(jnp.int32, (tq, e), 1)
        aug = jnp.where(lane < 2, slope, 0.0)
        rows = [jnp.concatenate([jnp.where(sel, q, 0.0), aug], axis=1)
                for sel in (lane < DA_QK_DIM, lane >= DA_QK_DIM)]
        qboth = jnp.concatenate(rows, axis=0).astype(BF16)
        for cb in range(ncb):
            qaug_ref[cb] = qboth[cb * ATT_CB:(cb + 1) * ATT_CB]
        m_ref[...] = jnp.full_like(m_ref, NEG_BIG)
        acc_ref[...] = jnp.zeros_like(acc_ref)

    off = slope * jnp.full((1, ATT_CB), (ki - qi) * tk, jnp.int32).astype(F32)

    def run(masked):
        kaug = kaug_ref[...]
        v_t = vt_ref[...]

        ahead = 1
        pending = [_dot_nt(kaug, qaug_ref[cb]) for cb in range(ahead)]
        for cb in range(ncb):
            s = pending.pop(0)
            if cb + ahead < ncb:
                pending.append(_dot_nt(kaug, qaug_ref[cb + ahead]))
            if masked:
                col = cb * ATT_CB + lax.broadcasted_iota(jnp.int32, (1, ATT_CB), 1)
                ql = jnp.where(col >= tq, col - tq, col)
                kl = lax.broadcasted_iota(jnp.int32, (tk, 1), 0)
                s = jnp.where(kl <= ql, s, NEG_BIG)
            m_prev = m_ref[cb]
            m_new = jnp.maximum(m_prev, jnp.max(s, axis=0, keepdims=True) + off)
            alpha = jnp.exp(m_prev - m_new)
            pm = jnp.exp(s - (m_new - off)).astype(BF16)
            acc_ref[cb] = alpha * acc_ref[cb] + _dot(v_t, pm)
            m_ref[cb] = m_new

    @pl.when(ki != qi)
    def _():
        run(False)

    @pl.when(ki == qi)
    def _():
        run(True)
        lam = (jnp.exp(jnp.sum(lq1_ref[...] * lk1_ref[...], axis=-1, keepdims=True))
               - jnp.exp(jnp.sum(lq2_ref[...] * lk2_ref[...], axis=-1, keepdims=True)) + lambda_init)
        acc = jnp.concatenate([acc_ref[cb, 0:e] for cb in range(ncb)], axis=1)
        den = jnp.concatenate([acc_ref[cb, e:e + 1] for cb in range(ncb)], axis=1)
        o = acc / den
        o = o[:, :tq] - lam * o[:, tq:]
        ms = jnp.mean(o * o, axis=0, keepdims=True)
        o = o * lax.rsqrt(ms + 1e-5) * (g_ref[...] * (1.0 - lambda_init))
        o_ref[...] = o.T.astype(BF16)


def _diff_attention(z, nb, lq1, lk1, lq2, lk2, subln_g, lambda_init, col0, tq=ROW_TILE):
    n = z.shape[0]
    nq = n // nb // tq
    pairs = [(qi, ki) for qi in range(nq) for ki in range(qi + 1)]
    qi_tab = jnp.asarray([p[0] for p in pairs], jnp.int32)
    ki_tab = jnp.asarray([p[1] for p in pairs], jnp.int32)
    slopes = 2.0 ** (-8.0 * jnp.arange(1, DA_HEADS + 1, dtype=F32) / DA_HEADS)
    e = DA_HEAD_DIM
    assert (2 * tq) % ATT_CB == 0 and tq % LANES == 0
    kaug, vt = _attn_prep(z, nb, col0, tq)
    cq = col0 // e
    const = lambda shape: pl.BlockSpec(shape, lambda b, h, p, qt, kt: (0, 0))
    grid_spec = pltpu.PrefetchScalarGridSpec(
        num_scalar_prefetch=2,
        grid=(nb, DA_HEADS, len(pairs)),
        in_specs=[
            pl.BlockSpec(memory_space=pltpu.SMEM),
            pl.BlockSpec((tq, e), lambda b, h, p, qt, kt: (b * nq + qt[p], cq + h)),
            pl.BlockSpec((None, None, tq, 2 * e), lambda b, h, p, qt, kt: (b, h, kt[p], 0)),
            pl.BlockSpec((None, None, e + SUBLANES, tq), lambda b, h, p, qt, kt: (b, h, 0, kt[p])),
            const((1, DA_QK_DIM)), const((1, DA_QK_DIM)), const((1, DA_QK_DIM)), const((1, DA_QK_DIM)),
            const((e, 1)),
        ],
        out_specs=pl.BlockSpec((tq, e), lambda b, h, p, qt, kt: (b * nq + qt[p], h)),
        scratch_shapes=[
            pltpu.VMEM((2 * tq // ATT_CB, ATT_CB, 2 * e), BF16),
            pltpu.VMEM((2 * tq // ATT_CB, 1, ATT_CB), F32),
            pltpu.VMEM((2 * tq // ATT_CB, e + SUBLANES, ATT_CB), F32),
        ],
    )
    r64 = lambda a: a.reshape(1, DA_QK_DIM)
    return pl.pallas_call(
        functools.partial(_attn_kernel, lambda_init=lambda_init),
        grid_spec=grid_spec,
        out_shape=jax.ShapeDtypeStruct((n, DA_WIDTH), BF16),
        compiler_params=_cparams(("parallel", "parallel", "arbitrary")),
        name="diff_attention",
    )(qi_tab, ki_tab, slopes, z, kaug, vt, r64(lq1), r64(lk1), r64(lq2), r64(lk2), subln_g.reshape(e, 1))


def _s5_disc_kernel(lr_ref, li_ref, ldt_ref, ar_ref, ai_ref, cr_ref, ci_ref):
    lr, li = lr_ref[...], li_ref[...]
    dt = jnp.exp(ldt_ref[...])
    mag = jnp.exp(lr * dt)
    ar = mag * jnp.cos(li * dt)
    ai = mag * jnp.sin(li * dt)
    den = lr * lr + li * li
    ar_ref[...] = ar
    ai_ref[...] = ai
    cr_ref[...] = ((ar - 1.0) * lr + ai * li) / den
    ci_ref[...] = (ai * lr - (ar - 1.0) * li) / den


def _s5_discretise(lam_re, lam_im, log_dt):
    shp = jax.ShapeDtypeStruct(lam_re.shape, F32)
    ldt = jnp.broadcast_to(log_dt[:, None], lam_re.shape)
    return pl.pallas_call(_s5_disc_kernel, out_shape=(shp, shp, shp, shp), name="s5_discretise")(
        lam_re, lam_im, ldt)


S5_LC = 8
S5_KB = 8
S5_KB_CH = S5_WIDTH // S5_KB
S5_KB_ST = S5_GROUPS // S5_KB * S5_STATE


def _s5_chunk_kernel(u_ref, t_ref, e_ref, f_ref, ar_ref, ai_ref, y_ref, g_ref, pr_ref, pi_ref):
    tm = u_ref.shape[0] // S5_LC
    ns = S5_KB_ST

    @pl.when(pl.program_id(2) == 0)
    def _():
        g_ref[0:SUBLANES, :] = jnp.zeros((SUBLANES, 2 * ns), F32)
        ar, ai = ar_ref[0], ai_ref[0]
        for _ in range(int(math.log2(S5_LC))):
            ar, ai = ar * ar - ai * ai, 2.0 * ar * ai
        pr, pi = ar, ai
        for j in range(SUBLANES):
            pr_ref[j:j + 1, :] = pr
            pi_ref[j:j + 1, :] = pi
            pr, pi = pr * ar - pi * ai, pr * ai + pi * ar

    x = jnp.concatenate([u_ref[pl.ds(i, tm, stride=S5_LC), :] for i in range(S5_LC)],
                        axis=1).astype(BF16)
    g_ref[SUBLANES:SUBLANES + tm, :] = _dot(x, e_ref[0])
    row = lax.broadcasted_iota(jnp.int32, (SUBLANES, ns), 0)

    def body(gi, carry):
        cr, ci = carry
        r0 = pl.multiple_of(SUBLANES + gi * SUBLANES, SUBLANES)
        xr = g_ref[pl.ds(r0, SUBLANES), 0:ns]
        xi = g_ref[pl.ds(r0, SUBLANES), ns:2 * ns]
        for s in (1, 2, 4):
            keep = row >= s
            ar = pr_ref[s - 1:s, :]
            ai = pi_ref[s - 1:s, :]
            sr = jnp.where(keep, pltpu.roll(xr, s, 0), 0.0)
            si = jnp.where(keep, pltpu.roll(xi, s, 0), 0.0)
            xr, xi = xr + ar * sr - ai * si, xi + ar * si + ai * sr
        pr, pi = pr_ref[...], pi_ref[...]
        xr, xi = xr + pr * cr - pi * ci, xi + pr * ci + pi * cr
        g_ref[pl.ds(r0, SUBLANES), 0:ns] = xr
        g_ref[pl.ds(r0, SUBLANES), ns:2 * ns] = xi
        return xr[SUBLANES - 1:SUBLANES, :], xi[SUBLANES - 1:SUBLANES, :]

    carry_in = (g_ref[SUBLANES - 1:SUBLANES, 0:ns], g_ref[SUBLANES - 1:SUBLANES, ns:2 * ns])
    lax.fori_loop(0, tm // SUBLANES, body, carry_in)
    h_start = g_ref[SUBLANES - 1:SUBLANES - 1 + tm, :].astype(BF16)
    y = _dot(x, t_ref[0]) + _dot(h_start, f_ref[0])
    for j in range(S5_LC):
        y_ref[pl.ds(j, tm, stride=S5_LC), :] = y[:, j * S5_KB_CH:(j + 1) * S5_KB_CH]
    g_ref[SUBLANES - 1:SUBLANES, :] = g_ref[SUBLANES - 1 + tm:SUBLANES + tm, :]


def _s5_chunked(u, nb, t_w, e_w, f_w, ar, ai, tm=208):
    n = u.shape[0]
    rows = n // S5_LC
    nt = rows // nb // tm
    assert nt * tm * nb == rows and tm % SUBLANES == 0 and S5_KB_CH == LANES
    kw = S5_LC * S5_KB_CH
    tile = pl.BlockSpec((tm * S5_LC, S5_KB_CH), lambda kb, b, t: (b * nt + t, kb))
    wspec = lambda k, m: pl.BlockSpec((1, k, m), lambda kb, b, t: (kb, 0, 0), pipeline_mode=pl.Buffered(1))
    aspec = pl.BlockSpec((1, 1, S5_KB_ST), lambda kb, b, t: (kb, 0, 0))
    return pl.pallas_call(
        _s5_chunk_kernel,
        grid=(S5_KB, nb, nt),
        in_specs=[tile, wspec(kw, kw), wspec(kw, 2 * S5_KB_ST), wspec(2 * S5_KB_ST, kw), aspec, aspec],
        out_specs=tile,
        out_shape=jax.ShapeDtypeStruct((n, S5_WIDTH), F32),
        scratch_shapes=[
            pltpu.VMEM((tm + SUBLANES, 2 * S5_KB_ST), F32),
            pltpu.VMEM((SUBLANES, S5_KB_ST), F32),
            pltpu.VMEM((SUBLANES, S5_KB_ST), F32),
        ],
        compiler_params=_cparams(("parallel", "parallel", "arbitrary")),
        name="s5_chunk",
    )(u, t_w, e_w, f_w, ar.reshape(S5_KB, 1, S5_KB_ST), ai.reshape(S5_KB, 1, S5_KB_ST))


def _s5_post_kernel(y_ref, u_ref, d_ref, gw_ref, gb_ref, o_ref):
    y = _gelu_tanh(y_ref[...] + d_ref[...] * u_ref[...])
    gate = jax.nn.sigmoid(_dot(y.astype(BF16), gw_ref[...]) + gb_ref[...])
    o_ref[...] = (y * gate).astype(BF16)


def _s5_post(y, u, d_skip, glu_w, glu_b):
    n, w = u.shape
    tm = ROW_TILE
    rows = pl.BlockSpec((tm, w), lambda i: (i, 0))
    full = lambda shape: pl.BlockSpec(shape, lambda i: (0,) * len(shape))
    return pl.pallas_call(
        _s5_post_kernel,
        grid=(n // tm,),
        in_specs=[rows, rows, full((1, w)), full((w, w)), full((1, w))],
        out_specs=rows,
        out_shape=jax.ShapeDtypeStruct((n, w), BF16),
        compiler_params=_cparams(("parallel",)),
        name="s5_post",
    )(y, u, d_skip.reshape(1, w), glu_w, glu_b.reshape(1, w))


def _token_shift(x_ref, mu_ref, xe_ref, first):
    tt = x_ref.shape[0]

    @pl.when(first)
    def _():
        xe_ref[0:SUBLANES, :] = jnp.zeros((SUBLANES, x_ref.shape[1]), F32)

    x = x_ref[...]
    xe_ref[SUBLANES:SUBLANES + tt, :] = x
    xprev = xe_ref[SUBLANES - 1:SUBLANES - 1 + tt, :]
    xe_ref[0:SUBLANES, :] = x_ref[tt - SUBLANES:tt, :]
    return x + (xprev - x) * mu_ref[...]


def _rw_prep_kernel(xr_ref, xk_ref, xv_ref, xl_ref, mur_ref, muk_ref, muv_ref, mul_ref,
                    w0_ref, w2_ref, a0_ref, a2_ref, g2_ref, kk_ref, ka_ref, ones_ref,
                    r_out, lw_out, k_out, v_out, kk_out, a_out, g_out,
                    er_ref, ek_ref, ev_ref, el_ref):
    first = pl.program_id(1) == 0
    r = _token_shift(xr_ref, mur_ref, er_ref, first)
    k = _token_shift(xk_ref, muk_ref, ek_ref, first)
    v = _token_shift(xv_ref, muv_ref, ev_ref, first)
    lo = _token_shift(xl_ref, mul_ref, el_ref, first)
    wl = lo[:, 0:RW_LORA_PAD]
    al = lo[:, RW_LORA_PAD:2 * RW_LORA_PAD]
    gl = lo[:, 2 * RW_LORA_PAD:2 * RW_LORA_PAD + RW_GATE_LORA]

    wdec = -_softplus(-(w0_ref[...] + _dot(jnp.tanh(wl).astype(BF16), w2_ref[...]))) - 0.5
    a = jax.nn.sigmoid(a0_ref[...] + _dot(al.astype(BF16), a2_ref[...]))
    g = _dot(jax.nn.sigmoid(gl).astype(BF16), g2_ref[...])
    kkr = k * kk_ref[...]
    ssq = _split_dot(kkr * kkr, ones_ref[...])
    kk = kkr / jnp.maximum(jnp.sqrt(ssq), 1e-12)

    r_out[...] = r
    lw_out[...] = -jnp.exp(wdec)
    k_out[...] = k * (1.0 + (a - 1.0) * ka_ref[...])
    v_out[...] = v
    kk_out[...] = kk
    a_out[...] = a
    g_out[...] = g


RW_LORA_COLS = 2 * RW_LORA_PAD + RW_GATE_LORA


def _rw_prep(z, nb, col0, mu, w0, w2, a0, a2, g2, k_k, k_a, ones_bd, tt=320):
    n = z.shape[0]
    nt = n // nb // tt
    w = RW_WIDTH
    lw = RW_LORA_COLS
    assert col0 % w == 0 and (col0 + 3 * w) % lw == 0
    cb = col0 // w
    cl = (col0 + 3 * w) // lw
    full = lambda shape: pl.BlockSpec(shape, lambda b, t: (0,) * len(shape))
    xblk = lambda width, c: pl.BlockSpec((tt, width), lambda b, t: (b * nt + t, c))
    out = jax.ShapeDtypeStruct((n, w), F32)
    ospec = pl.BlockSpec((tt, w), lambda b, t: (b * nt + t, 0))
    mu = mu.reshape(1, 3 * w + lw)
    return pl.pallas_call(
        _rw_prep_kernel,
        grid=(nb, nt),
        in_specs=[
            xblk(w, cb), xblk(w, cb + 1), xblk(w, cb + 2), xblk(lw, cl),
            full((1, w)), full((1, w)), full((1, w)), full((1, lw)),
            full((1, w)), full((RW_LORA_PAD, w)), full((1, w)), full((RW_LORA_PAD, w)),
            full((RW_GATE_LORA, w)), full((1, w)), full((1, w)), full((w, w)),
        ],
        out_specs=[ospec] * 7,
        out_shape=[out] * 7,
        scratch_shapes=[pltpu.VMEM((tt + SUBLANES, w), F32)] * 3 + [pltpu.VMEM((tt + SUBLANES, lw), F32)],
        compiler_params=_cparams(("parallel", "arbitrary")),
        name="rwkv_prep",
    )(z, z, z, z, mu[:, 0:w], mu[:, w:2 * w], mu[:, 2 * w:3 * w], mu[:, 3 * w:],
      w0.reshape(1, w), w2, a0.reshape(1, w), a2, g2, k_k.reshape(1, w), k_a.reshape(1, w), ones_bd)


def _rw_chunk_kernel(r_ref, lw_ref, k_ref, v_ref, kk_ref, a_ref, g_ref, rk_ref, lnw_ref, lnb_ref,
                     ones_ref, o_ref, s_ref):
    @pl.when(pl.program_id(1) == 0)
    def _():
        s_ref[...] = jnp.zeros_like(s_ref)

    gw = RW_GROUP * RW_HEAD_DIM
    nb, _, width = r_ref.shape
    chains = [(b, slice(j * gw, (j + 1) * gw)) for b in range(nb) for j in range(width // gw)]
    ins = [tuple(ref[b, :, sl] for ref in (r_ref, lw_ref, k_ref, v_ref, kk_ref, a_ref))
           for b, sl in chains]
    s0s = [s_ref[i] for i in range(len(chains))]
    ys, s_news = _rw_chunk_steps(ins, s0s)
    for i, (b, sl) in enumerate(chains):
        s_ref[i] = s_news[i]
        o_ref[b, :, sl] = _rw_output(ys[i], r_ref[b, :, sl], k_ref[b, :, sl], v_ref[b, :, sl],
                                     g_ref[b, :, sl], rk_ref[:, sl], lnw_ref[:, sl], lnb_ref[:, sl],
                                     ones_ref[...])


def _rw_chunk_prepare(r, lw, k, v, kk, a):
    L = RW_CHUNK
    gw = RW_GROUP * RW_HEAD_DIM

    ti = lax.broadcasted_iota(jnp.int32, (L, L), 0)
    tj = lax.broadcasted_iota(jnp.int32, (L, L), 1)
    tri = jnp.where(ti >= tj, 1.0, 0.0).astype(BF16)
    x1 = lw.astype(BF16)
    r1 = lw - x1.astype(F32)
    x2 = r1.astype(BF16)
    x3 = (r1 - x2.astype(F32)).astype(BF16)
    cl = _dot(tri, x1) + _dot(tri, x2) + _dot(tri, x3)
    g_last = jnp.exp(cl[L - 1:L, :])
    e_in = jnp.exp(cl)
    e_ex = jnp.exp(cl - lw)
    e_inv = jnp.exp(-cl)
    beta = kk * a
    a_t = -kk * e_ex
    r_t = r * e_in
    b_t = beta * e_inv
    k_t = k * e_inv

    lane_head = lax.shift_right_logical(lax.broadcasted_iota(jnp.int32, (L, gw), 1),
                                        int(math.log2(RW_HEAD_DIM)))

    def stack(x):
        return jnp.concatenate([jnp.where(lane_head == hh, x, 0.0) for hh in range(RW_GROUP)],
                               axis=0).astype(BF16)

    ar_s = jnp.concatenate([stack(a_t), stack(r_t)], axis=0)
    bk_s = jnp.concatenate([stack(b_t), stack(k_t)], axis=0)
    return dict(ar_s=ar_s, bk_s=bk_s, v_s=stack(v), bh_s=stack(b_t * g_last), kh_s=stack(k_t * g_last),
                g_last=g_last)


def _rw_chunk_steps(ins, s0s):
    L = RW_CHUNK
    rows = RW_GROUP * L
    n = range(len(ins))
    pre = [_rw_chunk_prepare(*x) for x in ins]
    prod = [_dot_nt(p["ar_s"], p["bk_s"]) for p in pre]
    w0 = [_dot_nt(p["ar_s"], s0.astype(BF16)) for p, s0 in zip(pre, s0s)]
    ri = lax.broadcasted_iota(jnp.int32, (rows, rows), 0)
    ci = lax.broadcasted_iota(jnp.int32, (rows, rows), 1)
    strict = ri > ci
    incl = ri >= ci
    npow = [jnp.where(strict, pr[:rows, :rows], 0.0).astype(BF16) for pr in prod]
    a_ak = [jnp.where(strict, pr[:rows, rows:], 0.0).astype(BF16) for pr in prod]
    m_rb = [jnp.where(incl, pr[rows:, :rows], 0.0).astype(BF16) for pr in prod]
    m_rk = [jnp.where(incl, pr[rows:, rows:], 0.0).astype(BF16) for pr in prod]
    x = [w0[i][:rows] + _dot(a_ak[i], pre[i]["v_s"]) for i in n]
    steps = int(math.log2(L))
    for kx in range(steps):
        x = [x[i] + _dot(npow[i], x[i].astype(BF16)) for i in n]
        if kx + 1 < steps:
            npow = [_dot(npow[i], npow[i]).astype(BF16) for i in n]
    u_s = [xi.astype(BF16) for xi in x]
    y_s = [w0[i][rows:] + _dot(m_rb[i], u_s[i]) + _dot(m_rk[i], pre[i]["v_s"]) for i in n]
    ys = []
    for ysi in y_s:
        y = ysi[0:L]
        for hh in range(1, RW_GROUP):
            y = y + ysi[hh * L:(hh + 1) * L]
        ys.append(y)
    s_new = [s0s[i] * pre[i]["g_last"] + _dot_tn(u_s[i], pre[i]["bh_s"])
             + _dot_tn(pre[i]["v_s"], pre[i]["kh_s"]) for i in n]
    return ys, s_new


def _rw_output(y, r, k, v, g, r_k, ln_w, ln_b, ones):
    inv_n = 1.0 / RW_HEAD_DIM
    mean = _split_dot(y, ones) * inv_n
    yc = y - mean
    var = _split_dot(yc * yc, ones) * inv_n
    yn = yc * lax.rsqrt(var + RW_GN_EPS) * ln_w + ln_b
    bonus = _split_dot(r * k * r_k, ones) * v
    return ((yn + bonus) * g).astype(BF16)


def _rw_chunk(prep, nb, r_k, ln_w, ln_b, ones_g):
    n = prep[0].shape[0]
    L = RW_CHUNK
    gw = RW_GROUP * RW_HEAD_DIM
    bw = RW_STEP_GROUPS * gw
    tp = n // nb
    w = RW_WIDTH
    blk = lambda: pl.BlockSpec((nb, L, bw), lambda gi, c: (0, c, gi))
    par = lambda: pl.BlockSpec((1, bw), lambda gi, c: (0, gi))
    out = pl.pallas_call(
        _rw_chunk_kernel,
        grid=(w // bw, tp // L),
        in_specs=[blk() for _ in range(7)] + [par(), par(), par(),
                                             pl.BlockSpec((gw, gw), lambda gi, c: (0, 0))],
        out_specs=blk(),
        out_shape=jax.ShapeDtypeStruct((nb, tp, w), BF16),
        scratch_shapes=[pltpu.VMEM((nb * RW_STEP_GROUPS, gw, gw), F32)],
        compiler_params=_cparams(("parallel", "arbitrary")),
        name="rwkv_chunk",
    )(*[x.reshape(nb, tp, w) for x in prep], r_k.reshape(1, w), ln_w.reshape(1, w), ln_b.reshape(1, w),
      ones_g)
    return out.reshape(n, w)


def _head_ones(width, head):
    idx = jnp.arange(width) // head
    return (idx[:, None] == idx[None, :]).astype(BF16)


def _pack_lru_gates(wa, wx):
    return jnp.concatenate([wa, wx], axis=-1).astype(BF16)


def _pack_odd_in(w_in, mu):
    s5 = S5_WIDTH
    c_rkv = s5 + 3 * RW_WIDTH
    c_wl = c_rkv + RW_DECAY_LORA
    c_al = c_wl + RW_AAA_LORA
    padw = lambda x, n: jnp.pad(x, ((0, 0), (0, n - x.shape[1])))
    w = jnp.concatenate([
        w_in[:, :c_rkv],
        padw(w_in[:, c_rkv:c_wl], RW_LORA_PAD),
        padw(w_in[:, c_wl:c_al], RW_LORA_PAD),
        w_in[:, c_al:],
    ], axis=1)
    m = mu[None, :]
    o = s5
    mu_p = jnp.concatenate([
        m[:, :c_rkv - o],
        padw(m[:, c_rkv - o:c_wl - o], RW_LORA_PAD),
        padw(m[:, c_wl - o:c_al - o], RW_LORA_PAD),
        m[:, c_al - o:],
    ], axis=1)[0]
    return w.astype(BF16), mu_p


def _pad_rows(x, n):
    return jnp.pad(x, ((0, n - x.shape[0]), (0, 0)))


def _pack_s5(ar, ai, cr, ci, b_re, b_im, c_re, c_im):
    lc, nk, gpb = S5_LC, S5_KB, S5_GROUPS // S5_KB
    hp = lax.Precision.HIGHEST
    bbr = cr[..., None] * b_re - ci[..., None] * b_im
    bbi = cr[..., None] * b_im + ci[..., None] * b_re
    pr, pi = [jnp.ones_like(ar)], [jnp.zeros_like(ai)]
    for _ in range(lc):
        pr, pi = pr + [pr[-1] * ar - pi[-1] * ai], pi + [pr[-1] * ai + pi[-1] * ar]
    pr, pi = jnp.stack(pr), jnp.stack(pi)
    car = c_re[None] * pr[:, :, None, :] - c_im[None] * pi[:, :, None, :]
    cai = c_re[None] * pi[:, :, None, :] + c_im[None] * pr[:, :, None, :]
    kt = (jnp.einsum("tgcp,gpd->tgcd", car[:lc], bbr, precision=hp)
          - jnp.einsum("tgcp,gpd->tgcd", cai[:lc], bbi, precision=hp))
    kt = jnp.concatenate([kt, jnp.zeros_like(kt[:1])], axis=0)
    pos = jnp.arange(lc)
    lag = pos[None, :] - pos[:, None]
    lag = jnp.where(lag >= 0, lag, lc)
    kw = lc * S5_KB_CH

    def widen(x):
        *lead, g, a, m = x.shape
        rep = jnp.tile(jnp.eye(m, dtype=BF16), (1, gpb))
        y = jnp.dot(x.astype(BF16).reshape(-1, m), rep).reshape(*lead, g * a, gpb * m)
        keep = (jnp.arange(g * a)[:, None] // a) == (jnp.arange(gpb * m)[None, :] // m)
        return jnp.where(keep, y, jnp.zeros((), BF16))

    blocks = widen(kt.reshape(lc + 1, nk, gpb, S5_GROUP, S5_GROUP).transpose(0, 1, 2, 4, 3))
    t_w = blocks[lag].transpose(2, 0, 3, 1, 4).reshape(nk, kw, kw)
    rev = lc - 1 - pos
    er = pr[rev][..., None] * bbr[None] - pi[rev][..., None] * bbi[None]
    ei = pr[rev][..., None] * bbi[None] + pi[rev][..., None] * bbr[None]

    def e_part(x):
        x = x.reshape(lc, nk, gpb, S5_STATE, S5_GROUP).transpose(1, 0, 2, 4, 3)
        return widen(x).reshape(nk, kw, S5_KB_ST)

    def f_part(x):
        x = x.reshape(lc, nk, gpb, S5_GROUP, S5_STATE).transpose(1, 0, 2, 4, 3)
        return widen(x).transpose(0, 2, 1, 3).reshape(nk, S5_KB_ST, kw)

    e_w = jnp.concatenate([e_part(er), e_part(ei)], axis=2)
    f_w = jnp.concatenate([f_part(car[1:]), f_part(-cai[1:])], axis=1)
    return t_w, e_w, f_w


def kernel(x, meta_tokens, norm_mix_g, norm_ffn_g, final_norm_g, ev_w_in, ev_conv_w, ev_conv_b, ev_lru_wa, ev_lru_ba, ev_lru_wx, ev_lru_bx, ev_lru_lambda, ev_lq1, ev_lk1, ev_lq2, ev_lk2, ev_subln_g, ev_w_out, od_w_in, od_s5_lam_re, od_s5_lam_im, od_s5_log_dt, od_s5_b_re, od_s5_b_im, od_s5_c_re, od_s5_c_im, od_s5_d, od_glu_w, od_glu_b, od_rw_mu, od_rw_w0, od_rw_w2, od_rw_a0, od_rw_a2, od_rw_g2, od_rw_kk, od_rw_ka, od_rw_rk, od_rw_ln_w, od_rw_ln_b, od_w_out, ffn_w_gate, ffn_w_up, ffn_w_down):
    nb, seq, d = x.shape
    depth = norm_mix_g.shape[0]
    t_real = N_META + seq
    tp = -(-t_real // ROW_TILE) * ROW_TILE
    meta = jnp.broadcast_to(meta_tokens[None].astype(x.dtype), (nb, N_META, d))
    pad = jnp.zeros((nb, tp - t_real, d), x.dtype)
    h = jnp.concatenate([meta, x, pad], axis=1).reshape(nb * tp, d)

    ones_rw = _head_ones(RW_WIDTH, RW_HEAD_DIM)
    ones_grp = _head_ones(RW_GROUP * RW_HEAD_DIM, RW_HEAD_DIM)

    for layer in range(depth):
        j = layer // 2
        if layer % 2 == 0:
            lambda_init = 0.8 - 0.6 * math.exp(-0.3 * layer)
            z = _rms_matmul(h, norm_mix_g[layer], ev_w_in[j].astype(BF16), tn=1280)
            ya = _lru_mixer(z, nb, ev_conv_w[j], ev_conv_b[j], _pack_lru_gates(ev_lru_wa[j], ev_lru_wx[j]),
                            ev_lru_ba[j], ev_lru_bx[j], ev_lru_lambda[j])
            yb = _diff_attention(z, nb, ev_lq1[j], ev_lk1[j], ev_lq2[j], ev_lk2[j], ev_subln_g[j],
                                 lambda_init, col0=2 * LRU_WIDTH)
            w_out = ev_w_out[j].astype(BF16)
            h = _out_proj(h, ya, yb, w_out[:LRU_WIDTH], w_out[LRU_WIDTH:])
        else:
            w_in, mu_p = _pack_odd_in(od_w_in[j], od_rw_mu[j])
            u = _rms_matmul(h, norm_mix_g[layer], w_in[:, :S5_WIDTH], tn=S5_WIDTH)
            z = _rms_matmul(h, norm_mix_g[layer], w_in[:, S5_WIDTH:], tn=(w_in.shape[1] - S5_WIDTH) // 2)
            ar, ai, cr, ci = _s5_discretise(od_s5_lam_re[j], od_s5_lam_im[j], od_s5_log_dt[j])
            t_w, e_w, f_w = _pack_s5(ar, ai, cr, ci, od_s5_b_re[j], od_s5_b_im[j], od_s5_c_re[j],
                                     od_s5_c_im[j])
            y8 = _s5_chunked(u, nb, t_w, e_w, f_w, ar, ai)
            yc = _s5_post(y8, u, od_s5_d[j], od_glu_w[j].astype(BF16), od_glu_b[j])
            prep = _rw_prep(z, nb, 0, mu_p, od_rw_w0[j],
                            _pad_rows(od_rw_w2[j], RW_LORA_PAD).astype(BF16), od_rw_a0[j],
                            _pad_rows(od_rw_a2[j], RW_LORA_PAD).astype(BF16), od_rw_g2[j].astype(BF16),
                            od_rw_kk[j], od_rw_ka[j], ones_rw)
            yd = _rw_chunk(prep, nb, od_rw_rk[j], od_rw_ln_w[j], od_rw_ln_b[j], ones_grp)
            w_out = od_w_out[j].astype(BF16)
            h = _out_proj(h, yc, yd, w_out[:S5_WIDTH], w_out[S5_WIDTH:])
        h = _ffn(h, norm_ffn_g[layer], ffn_w_gate[layer].astype(BF16), ffn_w_up[layer].astype(BF16),
                 ffn_w_down[layer].astype(BF16))
    return _final_norm(h, final_norm_g, nb, seq)
```

```python
import functools
import math

import jax
import jax.numpy as jnp
from jax import lax
from jax.experimental import pallas as pl
from jax.experimental.pallas import tpu as pltpu

F32 = jnp.float32
BF16 = jnp.bfloat16

D_MODEL = 2048
N_META = 16
NORM_EPS = 1e-6
LRU_WIDTH = 1024
LRU_BLOCKS = 8
LRU_BLOCK = 128
CONV_WIDTH = 4
LRU_C = 8.0
DA_WIDTH = 1024
DA_HEADS = 8
DA_HEAD_DIM = 128
DA_QK_DIM = 64
S5_WIDTH = 1024
S5_GROUP = 16
S5_GROUPS = 64
S5_STATE = 64
RW_WIDTH = 1024
RW_HEAD_DIM = 64
RW_HEADS = 16
RW_GN_EPS = 64e-5
RW_DECAY_LORA = 96
RW_AAA_LORA = 96
RW_GATE_LORA = 256
RW_LORA_PAD = 128
FFN_HIDDEN = 5632

SUBLANES = 8
LANES = 128
VMEM_LIMIT_BYTES = 56 * 1024 * 1024

ROW_TILE = 640
RW_CHUNK = 64
RW_GROUP = 4
RW_STEP_GROUPS = 2
NEG_BIG = -1e30


def _cparams(sem, flags=None):
    return pltpu.CompilerParams(dimension_semantics=sem, vmem_limit_bytes=VMEM_LIMIT_BYTES, flags=flags)


def _dot(a, b):
    return jnp.dot(a, b, preferred_element_type=F32)


def _dot_nt(a, b):
    return lax.dot_general(a, b, (((1,), (1,)), ((), ())), preferred_element_type=F32)


def _dot_tn(a, b):
    return lax.dot_general(a, b, (((0,), (0,)), ((), ())), preferred_element_type=F32)


def _split_dot(x, w_bf16):
    hi = x.astype(BF16)
    lo = (x - hi.astype(F32)).astype(BF16)
    return _dot(hi, w_bf16) + _dot(lo, w_bf16)


def _softplus(x):
    return jnp.maximum(x, 0.0) + jnp.log1p(jnp.exp(-jnp.abs(x)))


def _gelu_tanh(x):
    return x * (0.5 * (1.0 + jnp.tanh(0.7978845608028654 * (x + 0.044715 * (x * x * x)))))


def _rms_rows(x, g, eps):
    ms = jnp.mean(x * x, axis=-1, keepdims=True)
    return x * lax.rsqrt(ms + eps) * g


def _rms_matmul_kernel(h_ref, g_ref, w_ref, o_ref, hn_ref):
    @pl.when(pl.program_id(1) == 0)
    def _():
        hn_ref[...] = _rms_rows(h_ref[...], g_ref[...], NORM_EPS).astype(BF16)

    o_ref[...] = _dot(hn_ref[...], w_ref[...])


def _rms_matmul(h, g, w, tn=512):
    n, d = h.shape
    n_out = w.shape[1]
    tm = ROW_TILE
    return pl.pallas_call(
        _rms_matmul_kernel,
        grid=(n // tm, n_out // tn),
        in_specs=[
            pl.BlockSpec((tm, d), lambda i, j: (i, 0)),
            pl.BlockSpec((1, d), lambda i, j: (0, 0)),
            pl.BlockSpec((d, tn), lambda i, j: (0, j)),
        ],
        out_specs=pl.BlockSpec((tm, tn), lambda i, j: (i, j)),
        out_shape=jax.ShapeDtypeStruct((n, n_out), F32),
        scratch_shapes=[pltpu.VMEM((tm, d), BF16)],
        compiler_params=_cparams(("parallel", "arbitrary")),
        name="rms_matmul",
    )(h, g.reshape(1, d), w)


def _out_proj_kernel(h_ref, ya_ref, yb_ref, wa_ref, wb_ref, o_ref):
    o_ref[...] = (h_ref[...] + _dot(ya_ref[...].astype(BF16), wa_ref[...])
                  + _dot(yb_ref[...].astype(BF16), wb_ref[...]))


def _out_proj(h, ya, yb, w, layer, tn=1024):
    n, d = h.shape
    ka, kb = ya.shape[1], yb.shape[1]
    assert ka == kb and w.shape[1] == ka + kb
    tm = ROW_TILE
    return pl.pallas_call(
        _out_proj_kernel,
        grid=(n // tm, d // tn),
        in_specs=[
            pl.BlockSpec((tm, tn), lambda i, j: (i, j)),
            pl.BlockSpec((tm, ka), lambda i, j: (i, 0)),
            pl.BlockSpec((tm, kb), lambda i, j: (i, 0)),
            pl.BlockSpec((None, ka, tn), lambda i, j: (layer, 0, j)),
            pl.BlockSpec((None, kb, tn), lambda i, j: (layer, 1, j)),
        ],
        out_specs=pl.BlockSpec((tm, tn), lambda i, j: (i, j)),
        out_shape=jax.ShapeDtypeStruct((n, d), F32),
        compiler_params=_cparams(("parallel", "arbitrary")),
        name="out_proj",
    )(h, ya, yb, w, w)


def _ffn_kernel(h_ref, g_ref, wg_ref, wu_ref, wd_ref, o_ref, hn_ref, acc_ref):
    j = pl.program_id(1)

    @pl.when(j == 0)
    def _():
        hn_ref[...] = _rms_rows(h_ref[...], g_ref[...], NORM_EPS).astype(BF16)
        acc_ref[...] = jnp.zeros_like(acc_ref)

    hn = hn_ref[...]
    gate = _dot(hn, wg_ref[...])
    up = _dot(hn, wu_ref[...])
    act = (gate * jax.nn.sigmoid(gate) * up).astype(BF16)
    acc_ref[...] += _dot(act, wd_ref[...])

    @pl.when(j == pl.num_programs(1) - 1)
    def _():
        o_ref[...] = h_ref[...] + acc_ref[...]


def _ffn(h, g, wg, wu, wd, layer, th=512):
    n, d = h.shape
    hid = wg.shape[2]
    tm = ROW_TILE
    return pl.pallas_call(
        _ffn_kernel,
        grid=(n // tm, hid // th),
        in_specs=[
            pl.BlockSpec((tm, d), lambda i, j: (i, 0)),
            pl.BlockSpec((1, d), lambda i, j: (0, 0)),
            pl.BlockSpec((None, d, th), lambda i, j: (layer, 0, j)),
            pl.BlockSpec((None, d, th), lambda i, j: (layer, 0, j)),
            pl.BlockSpec((None, th, d), lambda i, j: (layer, j, 0)),
        ],
        out_specs=pl.BlockSpec((tm, d), lambda i, j: (i, 0)),
        out_shape=jax.ShapeDtypeStruct((n, d), F32),
        scratch_shapes=[pltpu.VMEM((tm, d), BF16), pltpu.VMEM((tm, d), F32)],
        compiler_params=_cparams(("parallel", "arbitrary")),
        name="ffn",
    )(h, g.reshape(1, d), wg, wu, wd)


def _final_norm_kernel(h_ref, g_ref, o_ref):
    o_ref[...] = _rms_rows(h_ref[...], g_ref[...], NORM_EPS)


def _final_norm(h, g, nb, seq, tm=512):
    n, d = h.shape
    tp = n // nb
    nt = seq // tm
    assert nt * tm == seq and N_META % SUBLANES == 0
    out = pl.pallas_call(
        _final_norm_kernel,
        grid=(nb, nt),
        in_specs=[pl.BlockSpec((pl.Element(tm), pl.Element(d)),
                               lambda b, i: (pl.multiple_of(b * tp + N_META + i * tm, SUBLANES), 0)),
                  pl.BlockSpec((1, d), lambda b, i: (0, 0))],
        out_specs=pl.BlockSpec((tm, d), lambda b, i: (b * nt + i, 0)),
        out_shape=jax.ShapeDtypeStruct((nb * seq, d), F32),
        compiler_params=_cparams(("parallel", "parallel")),
        name="final_norm",
    )(h, g.reshape(1, d))
    return out.reshape(nb, seq, d)


def _lru_kernel(xa_ref, ga_ref, cw_ref, cb_ref, wg_ref, ba_ref, bx_ref, lam_ref, o_ref,
                xe_ref, a_ref, b_ref, hc_ref):
    tt = xa_ref.shape[0]
    t = pl.program_id(1)

    @pl.when(t == 0)
    def _():
        xe_ref[0:SUBLANES, :] = jnp.zeros((SUBLANES, LRU_WIDTH), F32)
        hc_ref[...] = jnp.zeros_like(hc_ref)

    xe_ref[SUBLANES:SUBLANES + tt, :] = xa_ref[...]
    u = cb_ref[...]
    for j in range(CONV_WIDTH):
        off = SUBLANES - (CONV_WIDTH - 1) + j
        u = u + xe_ref[off:off + tt, :] * cw_ref[j:j + 1, :]
    xe_ref[0:SUBLANES, :] = xa_ref[tt - SUBLANES:tt, :]

    sp = _softplus(-lam_ref[...])
    for n in range(LRU_BLOCKS):
        sl = slice(n * LRU_BLOCK, (n + 1) * LRU_BLOCK)
        un = u[:, sl]
        zz = _dot(un.astype(BF16), wg_ref[n])
        r = jax.nn.sigmoid(zz[:, :LRU_BLOCK] + ba_ref[:, sl])
        i = jax.nn.sigmoid(zz[:, LRU_BLOCK:] + bx_ref[:, sl])
        log_a = (-LRU_C) * r * sp[:, sl]
        a = jnp.exp(log_a)
        gain = jnp.sqrt(-jnp.tanh(log_a) * (a * a + 1.0))
        a_ref[:, sl] = a
        b_ref[:, sl] = gain * (i * un)

    row = lax.broadcasted_iota(jnp.int32, (SUBLANES, LRU_WIDTH), 0)

    def body(gi, carry):
        r0 = pl.multiple_of(gi * SUBLANES, SUBLANES)
        a = a_ref[pl.ds(r0, SUBLANES), :]
        b = b_ref[pl.ds(r0, SUBLANES), :]
        for s in (1, 2, 4):
            keep = row >= s
            a_sh = jnp.where(keep, pltpu.roll(a, s, 0), 1.0)
            b_sh = jnp.where(keep, pltpu.roll(b, s, 0), 0.0)
            b = b + a * b_sh
            a = a * a_sh
        hblk = a * carry + b
        b_ref[pl.ds(r0, SUBLANES), :] = hblk
        return hblk[SUBLANES - 1:SUBLANES, :]

    hc_ref[...] = lax.fori_loop(0, tt // SUBLANES, body, hc_ref[...])
    o_ref[...] = (b_ref[...] * _gelu_tanh(ga_ref[...])).astype(BF16)


def _lru_mixer(z, nb, cw, cb, wg, ba, bx, lam, tt=ROW_TILE):
    n = z.shape[0]
    nt = n // nb // tt
    w = LRU_WIDTH
    row = lambda c: pl.BlockSpec((1, w), lambda b, t: (0, 0))
    return pl.pallas_call(
        _lru_kernel,
        grid=(nb, nt),
        in_specs=[
            pl.BlockSpec((tt, w), lambda b, t: (b * nt + t, 0)),
            pl.BlockSpec((tt, w), lambda b, t: (b * nt + t, 1)),
            pl.BlockSpec((CONV_WIDTH, w), lambda b, t: (0, 0)),
            row(0),
            pl.BlockSpec((LRU_BLOCKS, LRU_BLOCK, 2 * LRU_BLOCK), lambda b, t: (0, 0, 0)),
            row(0), row(0), row(0),
        ],
        out_specs=pl.BlockSpec((tt, w), lambda b, t: (b * nt + t, 0)),
        out_shape=jax.ShapeDtypeStruct((n, w), BF16),
        scratch_shapes=[
            pltpu.VMEM((tt + SUBLANES, w), F32),
            pltpu.VMEM((tt, w), F32),
            pltpu.VMEM((tt, w), F32),
            pltpu.VMEM((1, w), F32),
        ],
        compiler_params=_cparams(("parallel", "arbitrary")),
        name="lru_mixer",
    )(z, z, cw, cb.reshape(1, w), wg, ba.reshape(1, w), bx.reshape(1, w), lam.reshape(1, w))


ATT_CB = 256


def _attn_prep_kernel(k_ref, v_ref, posk_ref, kaug_ref, vt_ref):
    e = DA_HEAD_DIM
    tk = k_ref.shape[0]
    ones = jnp.ones((SUBLANES, tk), F32)
    for hh in range(DA_HEADS):
        sl = slice(hh * e, (hh + 1) * e)
        kaug_ref[0, hh, :, 0:e] = k_ref[:, sl].astype(BF16)
        kaug_ref[0, hh, :, e:2 * e] = posk_ref[...]
        vt_ref[0, hh] = jnp.concatenate([v_ref[:, sl].T, ones], axis=0).astype(BF16)


def _attn_prep(z, nb, col0, tk):
    n = z.shape[0]
    tp = n // nb
    nt = tp // tk
    e = DA_HEAD_DIM
    kl = jnp.arange(tk)
    hi_lo = jnp.stack([(kl // LANES) * LANES, kl % LANES], axis=1).astype(F32)
    posk = jnp.pad(hi_lo, ((0, 0), (0, e - 2))).astype(BF16)
    ck, cv = (col0 + DA_WIDTH) // DA_WIDTH, (col0 + 2 * DA_WIDTH) // DA_WIDTH
    return pl.pallas_call(
        _attn_prep_kernel,
        grid=(nb, nt),
        in_specs=[pl.BlockSpec((tk, DA_WIDTH), lambda b, t: (b * nt + t, ck)),
                  pl.BlockSpec((tk, DA_WIDTH), lambda b, t: (b * nt + t, cv)),
                  pl.BlockSpec((tk, e), lambda b, t: (0, 0))],
        out_specs=[pl.BlockSpec((1, DA_HEADS, tk, 2 * e), lambda b, t: (b, 0, t, 0)),
                   pl.BlockSpec((1, DA_HEADS, e + SUBLANES, tk), lambda b, t: (b, 0, 0, t))],
        out_shape=[jax.ShapeDtypeStruct((nb, DA_HEADS, tp, 2 * e), BF16),
                   jax.ShapeDtypeStruct((nb, DA_HEADS, e + SUBLANES, tp), BF16)],
        compiler_params=_cparams(("parallel", "parallel")),
        name="attn_prep",
    )(z, z, posk)


def _attn_kernel(qi_tab, ki_tab, slopes_ref, q_ref, kaug_ref, vt_ref, lq1_ref, lk1_ref, lq2_ref,
                 lk2_ref, g_ref, o_ref, qaug_ref, m_ref, acc_ref, *, lambda_init):
    tq = q_ref.shape[0]
    tk = kaug_ref.shape[0]
    e = DA_HEAD_DIM
    ncb = 2 * tq // ATT_CB
    h = pl.program_id(1)
    p = pl.program_id(2)
    qi = qi_tab[p]
    ki = ki_tab[p]
    slope = slopes_ref[h]

    @pl.when(ki == 0)
    def _():
        q = q_ref[...] * (DA_QK_DIM ** -0.5)
        lane = lax.broadcasted_iota(jnp.int32, (tq, e), 1)
        aug = jnp.where(lane < 2, slope, 0.0)
        rows = [jnp.concatenate([jnp.where(sel, q, 0.0), aug], axis=1)
                for sel in (lane < DA_QK_DIM, lane >= DA_QK_DIM)]
        qboth = jnp.concatenate(rows, axis=0).astype(BF16)
        for cb in range(ncb):
            qaug_ref[cb] = qboth[cb * ATT_CB:(cb + 1) * ATT_CB]
        m_ref[...] = jnp.full_like(m_ref, NEG_BIG)
        acc_ref[...] = jnp.zeros_like(acc_ref)

    off = slope * jnp.full((1, ATT_CB), (ki - qi) * tk, jnp.int32).astype(F32)

    def run(masked):
        kaug = kaug_ref[...]
        v_t = vt_ref[...]

        ahead = 1
        pending = [_dot_nt(kaug, qaug_ref[cb]) for cb in range(ahead)]
        for cb in range(ncb):
            s = pending.pop(0)
            if cb + ahead < ncb:
                pending.append(_dot_nt(kaug, qaug_ref[cb + ahead]))
            if masked:
                col = cb * ATT_CB + lax.broadcasted_iota(jnp.int32, (1, ATT_CB), 1)
                ql = jnp.where(col >= tq, col - tq, col)
                kl = lax.broadcasted_iota(jnp.int32, (tk, 1), 0)
                s = jnp.where(kl <= ql, s, NEG_BIG)
            m_prev = m_ref[cb]
            m_new = jnp.maximum(m_prev, jnp.max(s, axis=0, keepdims=True) + off)
            alpha = jnp.exp(m_prev - m_new)
            pm = jnp.exp(s - (m_new - off)).astype(BF16)
            acc_ref[cb] = alpha * acc_ref[cb] + _dot(v_t, pm)
            m_ref[cb] = m_new

    @pl.when(ki != qi)
    def _():
        run(False)

    @pl.when(ki == qi)
    def _():
        run(True)
        lam = (jnp.exp(jnp.sum(lq1_ref[...] * lk1_ref[...], axis=-1, keepdims=True))
               - jnp.exp(jnp.sum(lq2_ref[...] * lk2_ref[...], axis=-1, keepdims=True)) + lambda_init)
        acc = jnp.concatenate([acc_ref[cb, 0:e] for cb in range(ncb)], axis=1)
        den = jnp.concatenate([acc_ref[cb, e:e + 1] for cb in range(ncb)], axis=1)
        o = acc / den
        o = o[:, :tq] - lam * o[:, tq:]
        ms = jnp.mean(o * o, axis=0, keepdims=True)
        o = o * lax.rsqrt(ms + 1e-5) * (g_ref[...] * (1.0 - lambda_init))
        o_ref[...] = o.T.astype(BF16)


def _diff_attention(z, nb, lq1, lk1, lq2, lk2, subln_g, lambda_init, col0, tq=ROW_TILE):
    n = z.shape[0]
    nq = n // nb // tq
    pairs = [(qi, ki) for qi in range(nq) for ki in range(qi + 1)]
    qi_tab = jnp.asarray([p[0] for p in pairs], jnp.int32)
    ki_tab = jnp.asarray([p[1] for p in pairs], jnp.int32)
    slopes = 2.0 ** (-8.0 * jnp.arange(1, DA_HEADS + 1, dtype=F32) / DA_HEADS)
    e = DA_HEAD_DIM
    assert (2 * tq) % ATT_CB == 0 and tq % LANES == 0
    kaug, vt = _attn_prep(z, nb, col0, tq)
    cq = col0 // e
    const = lambda shape: pl.BlockSpec(shape, lambda b, h, p, qt, kt: (0, 0))
    grid_spec = pltpu.PrefetchScalarGridSpec(
        num_scalar_prefetch=2,
        grid=(nb, DA_HEADS, len(pairs)),
        in_specs=[
            pl.BlockSpec(memory_space=pltpu.SMEM),
            pl.BlockSpec((tq, e), lambda b, h, p, qt, kt: (b * nq + qt[p], cq + h)),
            pl.BlockSpec((None, None, tq, 2 * e), lambda b, h, p, qt, kt: (b, h, kt[p], 0)),
            pl.BlockSpec((None, None, e + SUBLANES, tq), lambda b, h, p, qt, kt: (b, h, 0, kt[p])),
            const((1, DA_QK_DIM)), const((1, DA_QK_DIM)), const((1, DA_QK_DIM)), const((1, DA_QK_DIM)),
            const((e, 1)),
        ],
        out_specs=pl.BlockSpec((tq, e), lambda b, h, p, qt, kt: (b * nq + qt[p], h)),
        scratch_shapes=[
            pltpu.VMEM((2 * tq // ATT_CB, ATT_CB, 2 * e), BF16),
            pltpu.VMEM((2 * tq // ATT_CB, 1, ATT_CB), F32),
            pltpu.VMEM((2 * tq // ATT_CB, e + SUBLANES, ATT_CB), F32),
        ],
    )
    r64 = lambda a: a.reshape(1, DA_QK_DIM)
    return pl.pallas_call(
        functools.partial(_attn_kernel, lambda_init=lambda_init),
        grid_spec=grid_spec,
        out_shape=jax.ShapeDtypeStruct((n, DA_WIDTH), BF16),
        compiler_params=_cparams(("parallel", "parallel", "arbitrary")),
        name="diff_attention",
    )(qi_tab, ki_tab, slopes, z, kaug, vt, r64(lq1), r64(lk1), r64(lq2), r64(lk2), subln_g.reshape(e, 1))


def _s5_disc_kernel(lr_ref, li_ref, ldt_ref, ar_ref, ai_ref, cr_ref, ci_ref):
    lr, li = lr_ref[...], li_ref[...]
    dt = jnp.exp(ldt_ref[...])
    mag = jnp.exp(lr * dt)
    ar = mag * jnp.cos(li * dt)
    ai = mag * jnp.sin(li * dt)
    den = lr * lr + li * li
    ar_ref[...] = ar
    ai_ref[...] = ai
    cr_ref[...] = ((ar - 1.0) * lr + ai * li) / den
    ci_ref[...] = (ai * lr - (ar - 1.0) * li) / den


def _s5_discretise(lam_re, lam_im, log_dt):
    shp = jax.ShapeDtypeStruct(lam_re.shape, F32)
    ldt = jnp.broadcast_to(log_dt[:, None], lam_re.shape)
    return pl.pallas_call(_s5_disc_kernel, out_shape=(shp, shp, shp, shp), name="s5_discretise")(
        lam_re, lam_im, ldt)


S5_LC = 8
S5_KB = 8
S5_KB_CH = S5_WIDTH // S5_KB
S5_KB_ST = S5_GROUPS // S5_KB * S5_STATE


def _s5_chunk_kernel(u_ref, t_ref, e_ref, f_ref, ar_ref, ai_ref, y_ref, g_ref, pr_ref, pi_ref):
    tm = u_ref.shape[0] // S5_LC
    ns = S5_KB_ST

    @pl.when(pl.program_id(2) == 0)
    def _():
        g_ref[0:SUBLANES, :] = jnp.zeros((SUBLANES, 2 * ns), F32)
        ar, ai = ar_ref[0], ai_ref[0]
        for _ in range(int(math.log2(S5_LC))):
            ar, ai = ar * ar - ai * ai, 2.0 * ar * ai
        pr, pi = ar, ai
        for j in range(SUBLANES):
            pr_ref[j:j + 1, :] = pr
            pi_ref[j:j + 1, :] = pi
            pr, pi = pr * ar - pi * ai, pr * ai + pi * ar

    x = jnp.concatenate([u_ref[pl.ds(i, tm, stride=S5_LC), :] for i in range(S5_LC)],
                        axis=1).astype(BF16)
    g_ref[SUBLANES:SUBLANES + tm, :] = _dot(x, e_ref[0])
    row = lax.broadcasted_iota(jnp.int32, (SUBLANES, ns), 0)

    def body(gi, carry):
        cr, ci = carry
        r0 = pl.multiple_of(SUBLANES + gi * SUBLANES, SUBLANES)
        xr = g_ref[pl.ds(r0, SUBLANES), 0:ns]
        xi = g_ref[pl.ds(r0, SUBLANES), ns:2 * ns]
        for s in (1, 2, 4):
            keep = row >= s
            ar = pr_ref[s - 1:s, :]
            ai = pi_ref[s - 1:s, :]
            sr = jnp.where(keep, pltpu.roll(xr, s, 0), 0.0)
            si = jnp.where(keep, pltpu.roll(xi, s, 0), 0.0)
            xr, xi = xr + ar * sr - ai * si, xi + ar * si + ai * sr
        pr, pi = pr_ref[...], pi_ref[...]
        xr, xi = xr + pr * cr - pi * ci, xi + pr * ci + pi * cr
        g_ref[pl.ds(r0, SUBLANES), 0:ns] = xr
        g_ref[pl.ds(r0, SUBLANES), ns:2 * ns] = xi
        return xr[SUBLANES - 1:SUBLANES, :], xi[SUBLANES - 1:SUBLANES, :]

    carry_in = (g_ref[SUBLANES - 1:SUBLANES, 0:ns], g_ref[SUBLANES - 1:SUBLANES, ns:2 * ns])
    lax.fori_loop(0, tm // SUBLANES, body, carry_in)
    h_start = g_ref[SUBLANES - 1:SUBLANES - 1 + tm, :].astype(BF16)
    y = _dot(x, t_ref[0]) + _dot(h_start, f_ref[0])
    for j in range(S5_LC):
        y_ref[pl.ds(j, tm, stride=S5_LC), :] = y[:, j * S5_KB_CH:(j + 1) * S5_KB_CH]
    g_ref[SUBLANES - 1:SUBLANES, :] = g_ref[SUBLANES - 1 + tm:SUBLANES + tm, :]


def _s5_chunked(u, nb, t_w, e_w, f_w, ar, ai, tm=208):
    n = u.shape[0]
    rows = n // S5_LC
    nt = rows // nb // tm
    assert nt * tm * nb == rows and tm % SUBLANES == 0 and S5_KB_CH == LANES
    kw = S5_LC * S5_KB_CH
    tile = pl.BlockSpec((tm * S5_LC, S5_KB_CH), lambda kb, b, t: (b * nt + t, kb))
    wspec = lambda k, m: pl.BlockSpec((1, k, m), lambda kb, b, t: (kb, 0, 0), pipeline_mode=pl.Buffered(1))
    aspec = pl.BlockSpec((1, 1, S5_KB_ST), lambda kb, b, t: (kb, 0, 0))
    return pl.pallas_call(
        _s5_chunk_kernel,
        grid=(S5_KB, nb, nt),
        in_specs=[tile, wspec(kw, kw), wspec(kw, 2 * S5_KB_ST), wspec(2 * S5_KB_ST, kw), aspec, aspec],
        out_specs=tile,
        out_shape=jax.ShapeDtypeStruct((n, S5_WIDTH), F32),
        scratch_shapes=[
            pltpu.VMEM((tm + SUBLANES, 2 * S5_KB_ST), F32),
            pltpu.VMEM((SUBLANES, S5_KB_ST), F32),
            pltpu.VMEM((SUBLANES, S5_KB_ST), F32),
        ],
        compiler_params=_cparams(("parallel", "parallel", "arbitrary")),
        name="s5_chunk",
    )(u, t_w, e_w, f_w, ar.reshape(S5_KB, 1, S5_KB_ST), ai.reshape(S5_KB, 1, S5_KB_ST))


def _s5_post_kernel(y_ref, u_ref, d_ref, gw_ref, gb_ref, o_ref):
    y = _gelu_tanh(y_ref[...] + d_ref[...] * u_ref[...])
    gate = jax.nn.sigmoid(_dot(y.astype(BF16), gw_ref[...]) + gb_ref[...])
    o_ref[...] = (y * gate).astype(BF16)


def _s5_post(y, u, d_skip, glu_w, glu_b):
    n, w = u.shape
    tm = ROW_TILE
    rows = pl.BlockSpec((tm, w), lambda i: (i, 0))
    full = lambda shape: pl.BlockSpec(shape, lambda i: (0,) * len(shape))
    return pl.pallas_call(
        _s5_post_kernel,
        grid=(n // tm,),
        in_specs=[rows, rows, full((1, w)), full((w, w)), full((1, w))],
        out_specs=rows,
        out_shape=jax.ShapeDtypeStruct((n, w), BF16),
        compiler_params=_cparams(("parallel",)),
        name="s5_post",
    )(y, u, d_skip.reshape(1, w), glu_w, glu_b.reshape(1, w))


def _token_shift(x_ref, mu_ref, xe_ref, first):
    tt = x_ref.shape[0]

    @pl.when(first)
    def _():
        xe_ref[0:SUBLANES, :] = jnp.zeros((SUBLANES, x_ref.shape[1]), F32)

    x = x_ref[...]
    xe_ref[SUBLANES:SUBLANES + tt, :] = x
    xprev = xe_ref[SUBLANES - 1:SUBLANES - 1 + tt, :]
    xe_ref[0:SUBLANES, :] = x_ref[tt - SUBLANES:tt, :]
    return x + (xprev - x) * mu_ref[...]


def _rw_prep_kernel(xr_ref, xk_ref, xv_ref, xl_ref, mur_ref, muk_ref, muv_ref, mul_ref,
                    w0_ref, w2_ref, a0_ref, a2_ref, g2_ref, kk_ref, ka_ref, ones_ref,
                    r_out, lw_out, k_out, v_out, kk_out, a_out, g_out,
                    er_ref, ek_ref, ev_ref, el_ref):
    first = pl.program_id(1) == 0
    r = _token_shift(xr_ref, mur_ref, er_ref, first)
    k = _token_shift(xk_ref, muk_ref, ek_ref, first)
    v = _token_shift(xv_ref, muv_ref, ev_ref, first)
    lo = _token_shift(xl_ref, mul_ref, el_ref, first)
    wl = lo[:, 0:RW_LORA_PAD]
    al = lo[:, RW_LORA_PAD:2 * RW_LORA_PAD]
    gl = lo[:, 2 * RW_LORA_PAD:2 * RW_LORA_PAD + RW_GATE_LORA]

    wdec = -_softplus(-(w0_ref[...] + _dot(jnp.tanh(wl).astype(BF16), w2_ref[...]))) - 0.5
    a = jax.nn.sigmoid(a0_ref[...] + _dot(al.astype(BF16), a2_ref[...]))
    g = _dot(jax.nn.sigmoid(gl).astype(BF16), g2_ref[...])
    kkr = k * kk_ref[...]
    ssq = _split_dot(kkr * kkr, ones_ref[...])
    kk = kkr / jnp.maximum(jnp.sqrt(ssq), 1e-12)

    r_out[...] = r
    lw_out[...] = -jnp.exp(wdec)
    k_out[...] = k * (1.0 + (a - 1.0) * ka_ref[...])
    v_out[...] = v
    kk_out[...] = kk
    a_out[...] = a
    g_out[...] = g


RW_LORA_COLS = 2 * RW_LORA_PAD + RW_GATE_LORA


def _rw_prep(z, nb, col0, mu, w0, w2, a0, a2, g2, k_k, k_a, ones_bd, tt=320):
    n = z.shape[0]
    nt = n // nb // tt
    w = RW_WIDTH
    lw = RW_LORA_COLS
    assert col0 % w == 0 and (col0 + 3 * w) % lw == 0
    cb = col0 // w
    cl = (col0 + 3 * w) // lw
    full = lambda shape: pl.BlockSpec(shape, lambda b, t: (0,) * len(shape))
    xblk = lambda width, c: pl.BlockSpec((tt, width), lambda b, t: (b * nt + t, c))
    out = jax.ShapeDtypeStruct((n, w), F32)
    ospec = pl.BlockSpec((tt, w), lambda b, t: (b * nt + t, 0))
    mu = mu.reshape(1, 3 * w + lw)
    return pl.pallas_call(
        _rw_prep_kernel,
        grid=(nb, nt),
        in_specs=[
            xblk(w, cb), xblk(w, cb + 1), xblk(w, cb + 2), xblk(lw, cl),
            full((1, w)), full((1, w)), full((1, w)), full((1, lw)),
            full((1, w)), full((RW_LORA_PAD, w)), full((1, w)), full((RW_LORA_PAD, w)),
            full((RW_GATE_LORA, w)), full((1, w)), full((1, w)), full((w, w)),
        ],
        out_specs=[ospec] * 7,
        out_shape=[out] * 7,
        scratch_shapes=[pltpu.VMEM((tt + SUBLANES, w), F32)] * 3 + [pltpu.VMEM((tt + SUBLANES, lw), F32)],
        compiler_params=_cparams(("parallel", "arbitrary")),
        name="rwkv_prep",
    )(z, z, z, z, mu[:, 0:w], mu[:, w:2 * w], mu[:, 2 * w:3 * w], mu[:, 3 * w:],
      w0.reshape(1, w), w2, a0.reshape(1, w), a2, g2, k_k.reshape(1, w), k_a.reshape(1, w), ones_bd)


def _rw_chunk_kernel(r_ref, lw_ref, k_ref, v_ref, kk_ref, a_ref, g_ref, rk_ref, lnw_ref, lnb_ref,
                     ones_ref, o_ref, s_ref):
    @pl.when(pl.program_id(1) == 0)
    def _():
        s_ref[...] = jnp.zeros_like(s_ref)

    gw = RW_GROUP * RW_HEAD_DIM
    nb, _, width = r_ref.shape
    chains = [(b, slice(j * gw, (j + 1) * gw)) for b in range(nb) for j in range(width // gw)]
    ins = [tuple(ref[b, :, sl] for ref in (r_ref, lw_ref, k_ref, v_ref, kk_ref, a_ref))
           for b, sl in chains]
    s0s = [s_ref[i] for i in range(len(chains))]
    ys, s_news = _rw_chunk_steps(ins, s0s)
    for i, (b, sl) in enumerate(chains):
        s_ref[i] = s_news[i]
        o_ref[b, :, sl] = _rw_output(ys[i], r_ref[b, :, sl], k_ref[b, :, sl], v_ref[b, :, sl],
                                     g_ref[b, :, sl], rk_ref[:, sl], lnw_ref[:, sl], lnb_ref[:, sl],
                                     ones_ref[...])


def _rw_chunk_prepare(r, lw, k, v, kk, a):
    L = RW_CHUNK
    gw = RW_GROUP * RW_HEAD_DIM

    ti = lax.broadcasted_iota(jnp.int32, (L, L), 0)
    tj = lax.broadcasted_iota(jnp.int32, (L, L), 1)
    tri = jnp.where(ti >= tj, 1.0, 0.0).astype(BF16)
    x1 = lw.astype(BF16)
    r1 = lw - x1.astype(F32)
    x2 = r1.astype(BF16)
    x3 = (r1 - x2.astype(F32)).astype(BF16)
    cl = _dot(tri, x1) + _dot(tri, x2) + _dot(tri, x3)
    g_last = jnp.exp(cl[L - 1:L, :])
    e_in = jnp.exp(cl)
    e_ex = jnp.exp(cl - lw)
    e_inv = jnp.exp(-cl)
    beta = kk * a
    a_t = -kk * e_ex
    r_t = r * e_in
    b_t = beta * e_inv
    k_t = k * e_inv

    lane_head = lax.shift_right_logical(lax.broadcasted_iota(jnp.int32, (L, gw), 1),
                                        int(math.log2(RW_HEAD_DIM)))

    def stack(x):
        return jnp.concatenate([jnp.where(lane_head == hh, x, 0.0) for hh in range(RW_GROUP)],
                               axis=0).astype(BF16)

    ar_s = jnp.concatenate([stack(a_t), stack(r_t)], axis=0)
    bk_s = jnp.concatenate([stack(b_t), stack(k_t)], axis=0)
    return dict(ar_s=ar_s, bk_s=bk_s, v_s=stack(v), bh_s=stack(b_t * g_last), kh_s=stack(k_t * g_last),
                g_last=g_last)


def _rw_chunk_steps(ins, s0s):
    L = RW_CHUNK
    rows = RW_GROUP * L
    n = range(len(ins))
    pre = [_rw_chunk_prepare(*x) for x in ins]
    prod = [_dot_nt(p["ar_s"], p["bk_s"]) for p in pre]
    w0 = [_dot_nt(p["ar_s"], s0.astype(BF16)) for p, s0 in zip(pre, s0s)]
    ri = lax.broadcasted_iota(jnp.int32, (rows, rows), 0)
    ci = lax.broadcasted_iota(jnp.int32, (rows, rows), 1)
    strict = ri > ci
    incl = ri >= ci
    npow = [jnp.where(strict, pr[:rows, :rows], 0.0).astype(BF16) for pr in prod]
    a_ak = [jnp.where(strict, pr[:rows, rows:], 0.0).astype(BF16) for pr in prod]
    m_rb = [jnp.where(incl, pr[rows:, :rows], 0.0).astype(BF16) for pr in prod]
    m_rk = [jnp.where(incl, pr[rows:, rows:], 0.0).astype(BF16) for pr in prod]
    x = [w0[i][:rows] + _dot(a_ak[i], pre[i]["v_s"]) for i in n]
    steps = int(math.log2(L))
    for kx in range(steps):
        x = [x[i] + _dot(npow[i], x[i].astype(BF16)) for i in n]
        if kx + 1 < steps:
            npow = [_dot(npow[i], npow[i]).astype(BF16) for i in n]
    u_s = [xi.astype(BF16) for xi in x]
    y_s = [w0[i][rows:] + _dot(m_rb[i], u_s[i]) + _dot(m_rk[i], pre[i]["v_s"]) for i in n]
    ys = []
    for ysi in y_s:
        y = ysi[0:L]
        for hh in range(1, RW_GROUP):
            y = y + ysi[hh * L:(hh + 1) * L]
        ys.append(y)
    s_new = [s0s[i] * pre[i]["g_last"] + _dot_tn(u_s[i], pre[i]["bh_s"])
             + _dot_tn(pre[i]["v_s"], pre[i]["kh_s"]) for i in n]
    return ys, s_new


def _rw_output(y, r, k, v, g, r_k, ln_w, ln_b, ones):
    inv_n = 1.0 / RW_HEAD_DIM
    mean = _split_dot(y, ones) * inv_n
    yc = y - mean
    var = _split_dot(yc * yc, ones) * inv_n
    yn = yc * lax.rsqrt(var + RW_GN_EPS) * ln_w + ln_b
    bonus = _split_dot(r * k * r_k, ones) * v
    return ((yn + bonus) * g).astype(BF16)


def _rw_chunk(prep, nb, r_k, ln_w, ln_b, ones_g):
    n = prep[0].shape[0]
    L = RW_CHUNK
    gw = RW_GROUP * RW_HEAD_DIM
    bw = RW_STEP_GROUPS * gw
    tp = n // nb
    w = RW_WIDTH
    blk = lambda: pl.BlockSpec((nb, L, bw), lambda gi, c: (0, c, gi))
    par = lambda: pl.BlockSpec((1, bw), lambda gi, c: (0, gi))
    out = pl.pallas_call(
        _rw_chunk_kernel,
        grid=(w // bw, tp // L),
        in_specs=[blk() for _ in range(7)] + [par(), par(), par(),
                                             pl.BlockSpec((gw, gw), lambda gi, c: (0, 0))],
        out_specs=blk(),
        out_shape=jax.ShapeDtypeStruct((nb, tp, w), BF16),
        scratch_shapes=[pltpu.VMEM((nb * RW_STEP_GROUPS, gw, gw), F32)],
        compiler_params=_cparams(("parallel", "arbitrary")),
        name="rwkv_chunk",
    )(*[x.reshape(nb, tp, w) for x in prep], r_k.reshape(1, w), ln_w.reshape(1, w), ln_b.reshape(1, w),
      ones_g)
    return out.reshape(n, w)


def _head_ones(width, head):
    idx = jnp.arange(width) // head
    return (idx[:, None] == idx[None, :]).astype(BF16)


def _pack_lru_gates(wa, wx):
    return jnp.concatenate([wa, wx], axis=-1).astype(BF16)


def _pack_odd_in(w_in, mu):
    s5 = S5_WIDTH
    c_rkv = s5 + 3 * RW_WIDTH
    c_wl = c_rkv + RW_DECAY_LORA
    c_al = c_wl + RW_AAA_LORA
    padw = lambda x, n: jnp.pad(x, ((0, 0), (0, n - x.shape[1])))
    w = jnp.concatenate([
        w_in[:, :c_rkv],
        padw(w_in[:, c_rkv:c_wl], RW_LORA_PAD),
        padw(w_in[:, c_wl:c_al], RW_LORA_PAD),
        w_in[:, c_al:],
    ], axis=1)
    m = mu[None, :]
    o = s5
    mu_p = jnp.concatenate([
        m[:, :c_rkv - o],
        padw(m[:, c_rkv - o:c_wl - o], RW_LORA_PAD),
        padw(m[:, c_wl - o:c_al - o], RW_LORA_PAD),
        m[:, c_al - o:],
    ], axis=1)[0]
    return w.astype(BF16), mu_p


def _pad_rows(x, n):
    return jnp.pad(x, ((0, n - x.shape[0]), (0, 0)))


def _pack_s5(ar, ai, cr, ci, b_re, b_im, c_re, c_im):
    lc, nk, gpb = S5_LC, S5_KB, S5_GROUPS // S5_KB
    hp = lax.Precision.HIGHEST
    bbr = cr[..., None] * b_re - ci[..., None] * b_im
    bbi = cr[..., None] * b_im + ci[..., None] * b_re
    pr, pi = [jnp.ones_like(ar)], [jnp.zeros_like(ai)]
    for _ in range(lc):
        pr, pi = pr + [pr[-1] * ar - pi[-1] * ai], pi + [pr[-1] * ai + pi[-1] * ar]
    pr, pi = jnp.stack(pr), jnp.stack(pi)
    car = c_re[None] * pr[:, :, None, :] - c_im[None] * pi[:, :, None, :]
    cai = c_re[None] * pi[:, :, None, :] + c_im[None] * pr[:, :, None, :]
    kt = (jnp.einsum("tgcp,gpd->tgcd", car[:lc], bbr, precision=hp)
          - jnp.einsum("tgcp,gpd->tgcd", cai[:lc], bbi, precision=hp))
    kt = jnp.concatenate([kt, jnp.zeros_like(kt[:1])], axis=0)
    pos = jnp.arange(lc)
    lag = pos[None, :] - pos[:, None]
    lag = jnp.where(lag >= 0, lag, lc)
    kw = lc * S5_KB_CH

    def widen(x):
        *lead, g, a, m = x.shape
        rep = jnp.tile(jnp.eye(m, dtype=BF16), (1, gpb))
        y = jnp.dot(x.astype(BF16).reshape(-1, m), rep).reshape(*lead, g * a, gpb * m)
        keep = (jnp.arange(g * a)[:, None] // a) == (jnp.arange(gpb * m)[None, :] // m)
        return jnp.where(keep, y, jnp.zeros((), BF16))

    blocks = widen(kt.reshape(lc + 1, nk, gpb, S5_GROUP, S5_GROUP).transpose(0, 1, 2, 4, 3))
    t_w = blocks[lag].transpose(2, 0, 3, 1, 4).reshape(nk, kw, kw)
    rev = lc - 1 - pos
    er = pr[rev][..., None] * bbr[None] - pi[rev][..., None] * bbi[None]
    ei = pr[rev][..., None] * bbi[None] + pi[rev][..., None] * bbr[None]

    def e_part(x):
        x = x.reshape(lc, nk, gpb, S5_STATE, S5_GROUP).transpose(1, 0, 2, 4, 3)
        return widen(x).reshape(nk, kw, S5_KB_ST)

    def f_part(x):
        x = x.reshape(lc, nk, gpb, S5_GROUP, S5_STATE).transpose(1, 0, 2, 4, 3)
        return widen(x).transpose(0, 2, 1, 3).reshape(nk, S5_KB_ST, kw)

    e_w = jnp.concatenate([e_part(er), e_part(ei)], axis=2)
    f_w = jnp.concatenate([f_part(car[1:]), f_part(-cai[1:])], axis=1)
    return t_w, e_w, f_w


def kernel(x, meta_tokens, norm_mix_g, norm_ffn_g, final_norm_g, ev_w_in, ev_conv_w, ev_conv_b, ev_lru_wa, ev_lru_ba, ev_lru_wx, ev_lru_bx, ev_lru_lambda, ev_lq1, ev_lk1, ev_lq2, ev_lk2, ev_subln_g, ev_w_out, od_w_in, od_s5_lam_re, od_s5_lam_im, od_s5_log_dt, od_s5_b_re, od_s5_b_im, od_s5_c_re, od_s5_c_im, od_s5_d, od_glu_w, od_glu_b, od_rw_mu, od_rw_w0, od_rw_w2, od_rw_a0, od_rw_a2, od_rw_g2, od_rw_kk, od_rw_ka, od_rw_rk, od_rw_ln_w, od_rw_ln_b, od_w_out, ffn_w_gate, ffn_w_up, ffn_w_down):
    nb, seq, d = x.shape
    depth = norm_mix_g.shape[0]
    t_real = N_META + seq
    tp = -(-t_real // ROW_TILE) * ROW_TILE
    meta = jnp.broadcast_to(meta_tokens[None].astype(x.dtype), (nb, N_META, d))
    pad = jnp.zeros((nb, tp - t_real, d), x.dtype)
    h = jnp.concatenate([meta, x, pad], axis=1).reshape(nb * tp, d)

    ones_rw = _head_ones(RW_WIDTH, RW_HEAD_DIM)
    ones_grp = _head_ones(RW_GROUP * RW_HEAD_DIM, RW_HEAD_DIM)

    ffn_wg_b, ffn_wu_b, ffn_wd_b = (w.astype(BF16) for w in (ffn_w_gate, ffn_w_up, ffn_w_down))
    ev_w_out_b, od_w_out_b = ev_w_out.astype(BF16), od_w_out.astype(BF16)

    for layer in range(depth):
        j = layer // 2
        if layer % 2 == 0:
            lambda_init = 0.8 - 0.6 * math.exp(-0.3 * layer)
            z = _rms_matmul(h, norm_mix_g[layer], ev_w_in[j].astype(BF16), tn=1280)
            ya = _lru_mixer(z, nb, ev_conv_w[j], ev_conv_b[j], _pack_lru_gates(ev_lru_wa[j], ev_lru_wx[j]),
                            ev_lru_ba[j], ev_lru_bx[j], ev_lru_lambda[j])
            yb = _diff_attention(z, nb, ev_lq1[j], ev_lk1[j], ev_lq2[j], ev_lk2[j], ev_subln_g[j],
                                 lambda_init, col0=2 * LRU_WIDTH)
            h = _out_proj(h, ya, yb, ev_w_out_b, j)
        else:
            w_in, mu_p = _pack_odd_in(od_w_in[j], od_rw_mu[j])
            u = _rms_matmul(h, norm_mix_g[layer], w_in[:, :S5_WIDTH], tn=S5_WIDTH)
            z = _rms_matmul(h, norm_mix_g[layer], w_in[:, S5_WIDTH:], tn=(w_in.shape[1] - S5_WIDTH) // 2)
            ar, ai, cr, ci = _s5_discretise(od_s5_lam_re[j], od_s5_lam_im[j], od_s5_log_dt[j])
            t_w, e_w, f_w = _pack_s5(ar, ai, cr, ci, od_s5_b_re[j], od_s5_b_im[j], od_s5_c_re[j],
                                     od_s5_c_im[j])
            y8 = _s5_chunked(u, nb, t_w, e_w, f_w, ar, ai)
            yc = _s5_post(y8, u, od_s5_d[j], od_glu_w[j].astype(BF16), od_glu_b[j])
            prep = _rw_prep(z, nb, 0, mu_p, od_rw_w0[j],
                            _pad_rows(od_rw_w2[j], RW_LORA_PAD).astype(BF16), od_rw_a0[j],
                            _pad_rows(od_rw_a2[j], RW_LORA_PAD).astype(BF16), od_rw_g2[j].astype(BF16),
                            od_rw_kk[j], od_rw_ka[j], ones_rw)
            yd = _rw_chunk(prep, nb, od_rw_rk[j], od_rw_ln_w[j], od_rw_ln_b[j], ones_grp)
            h = _out_proj(h, yc, yd, od_w_out_b, j)
        h = _ffn(h, norm_ffn_g[layer], ffn_wg_b, ffn_wu_b, ffn_wd_b, layer)
    return _final_norm(h, final_norm_g, nb, seq)
```

```python
import functools
import math

import jax
import jax.numpy as jnp
from jax import lax
from jax.experimental import pallas as pl
from jax.experimental.pallas import tpu as pltpu

F32 = jnp.float32
BF16 = jnp.bfloat16

D_MODEL = 2048
N_META = 16
NORM_EPS = 1e-6
LRU_WIDTH = 1024
LRU_BLOCKS = 8
LRU_BLOCK = 128
CONV_WIDTH = 4
LRU_C = 8.0
DA_WIDTH = 1024
DA_HEADS = 8
DA_HEAD_DIM = 128
DA_QK_DIM = 64
S5_WIDTH = 1024
S5_GROUP = 16
S5_GROUPS = 64
S5_STATE = 64
RW_WIDTH = 1024
RW_HEAD_DIM = 64
RW_HEADS = 16
RW_GN_EPS = 64e-5
RW_DECAY_LORA = 96
RW_AAA_LORA = 96
RW_GATE_LORA = 256
RW_LORA_PAD = 128
FFN_HIDDEN = 5632

SUBLANES = 8
LANES = 128
VMEM_LIMIT_BYTES = 56 * 1024 * 1024

ROW_TILE = 640
RW_CHUNK = 64
RW_GROUP = 4
RW_STEP_GROUPS = 2
NEG_BIG = -1e30


def _cparams(sem, flags=None):
    return pltpu.CompilerParams(dimension_semantics=sem, vmem_limit_bytes=VMEM_LIMIT_BYTES, flags=flags)


def _dot(a, b):
    return jnp.dot(a, b, preferred_element_type=F32)


def _dot_nt(a, b):
    return lax.dot_general(a, b, (((1,), (1,)), ((), ())), preferred_element_type=F32)


def _dot_tn(a, b):
    return lax.dot_general(a, b, (((0,), (0,)), ((), ())), preferred_element_type=F32)


def _split_dot(x, w_bf16):
    hi = x.astype(BF16)
    lo = (x - hi.astype(F32)).astype(BF16)
    return _dot(hi, w_bf16) + _dot(lo, w_bf16)


def _softplus(x):
    return jnp.maximum(x, 0.0) + jnp.log1p(jnp.exp(-jnp.abs(x)))


def _gelu_tanh(x):
    return x * (0.5 * (1.0 + jnp.tanh(0.7978845608028654 * (x + 0.044715 * (x * x * x)))))


def _rms_rows(x, g, eps):
    ms = jnp.mean(x * x, axis=-1, keepdims=True)
    return x * lax.rsqrt(ms + eps) * g


def _rms_matmul_kernel(h_ref, g_ref, w_ref, o_ref, hn_ref):
    @pl.when(pl.program_id(1) == 0)
    def _():
        hn_ref[...] = _rms_rows(h_ref[...], g_ref[...], NORM_EPS).astype(BF16)

    o_ref[...] = _dot(hn_ref[...], w_ref[...])


def _rms_matmul(h, g, w, tn=512):
    n, d = h.shape
    n_out = w.shape[1]
    tm = ROW_TILE
    return pl.pallas_call(
        _rms_matmul_kernel,
        grid=(n // tm, n_out // tn),
        in_specs=[
            pl.BlockSpec((tm, d), lambda i, j: (i, 0)),
            pl.BlockSpec((1, d), lambda i, j: (0, 0)),
            pl.BlockSpec((d, tn), lambda i, j: (0, j)),
        ],
        out_specs=pl.BlockSpec((tm, tn), lambda i, j: (i, j)),
        out_shape=jax.ShapeDtypeStruct((n, n_out), F32),
        scratch_shapes=[pltpu.VMEM((tm, d), BF16)],
        compiler_params=_cparams(("parallel", "arbitrary")),
        name="rms_matmul",
    )(h, g.reshape(1, d), w)


def _mix_ffn_kernel(h_ref, ya_ref, yb_ref, wa_ref, wb_ref, g_ref, wg_ref, wu_ref, wd_ref, o_ref, hn_ref):
    @pl.when(pl.program_id(1) == 0)
    def _():
        hm = (h_ref[...] + _dot(ya_ref[...], wa_ref[...]) + _dot(yb_ref[...], wb_ref[...]))
        o_ref[...] = hm
        hn_ref[...] = _rms_rows(hm, g_ref[...], NORM_EPS).astype(BF16)

    hn = hn_ref[...]
    gate = _dot(hn, wg_ref[...])
    up = _dot(hn, wu_ref[...])
    act = (gate * jax.nn.sigmoid(gate) * up).astype(BF16)
    o_ref[...] += _dot(act, wd_ref[...])


def _mix_ffn(h, ya, yb, w_out, mix_layer, g, wg, wu, wd, layer, th=512):
    n, d = h.shape
    hid = wg.shape[2]
    ka, kb = ya.shape[1], yb.shape[1]
    assert ka == kb and w_out.shape[1] == ka + kb and ya.dtype == BF16 and yb.dtype == BF16
    tm = ROW_TILE
    rows = lambda width: pl.BlockSpec((tm, width), lambda i, j: (i, 0))
    once = pl.Buffered(1)
    return pl.pallas_call(
        _mix_ffn_kernel,
        grid=(n // tm, hid // th),
        in_specs=[
            rows(d), rows(ka), rows(kb),
            pl.BlockSpec((None, ka, d), lambda i, j: (mix_layer, 0, 0), pipeline_mode=once),
            pl.BlockSpec((None, kb, d), lambda i, j: (mix_layer, 1, 0), pipeline_mode=once),
            pl.BlockSpec((1, d), lambda i, j: (0, 0)),
            pl.BlockSpec((None, d, th), lambda i, j: (layer, 0, j)),
            pl.BlockSpec((None, d, th), lambda i, j: (layer, 0, j)),
            pl.BlockSpec((None, th, d), lambda i, j: (layer, j, 0)),
        ],
        out_specs=rows(d),
        out_shape=jax.ShapeDtypeStruct((n, d), F32),
        scratch_shapes=[pltpu.VMEM((tm, d), BF16)],
        compiler_params=_cparams(("parallel", "arbitrary")),
        name="mix_ffn",
    )(h, ya, yb, w_out, w_out, g.reshape(1, d), wg, wu, wd)


def _final_norm_kernel(h_ref, g_ref, o_ref):
    o_ref[...] = _rms_rows(h_ref[...], g_ref[...], NORM_EPS)


def _final_norm(h, g, nb, seq, tm=512):
    n, d = h.shape
    tp = n // nb
    nt = seq // tm
    assert nt * tm == seq and N_META % SUBLANES == 0
    out = pl.pallas_call(
        _final_norm_kernel,
        grid=(nb, nt),
        in_specs=[pl.BlockSpec((pl.Element(tm), pl.Element(d)),
                               lambda b, i: (pl.multiple_of(b * tp + N_META + i * tm, SUBLANES), 0)),
                  pl.BlockSpec((1, d), lambda b, i: (0, 0))],
        out_specs=pl.BlockSpec((tm, d), lambda b, i: (b * nt + i, 0)),
        out_shape=jax.ShapeDtypeStruct((nb * seq, d), F32),
        compiler_params=_cparams(("parallel", "parallel")),
        name="final_norm",
    )(h, g.reshape(1, d))
    return out.reshape(nb, seq, d)


def _lru_kernel(xa_ref, ga_ref, cw_ref, cb_ref, wg_ref, ba_ref, bx_ref, lam_ref, o_ref,
                xe_ref, a_ref, b_ref, hc_ref):
    tt = xa_ref.shape[0]
    t = pl.program_id(1)

    @pl.when(t == 0)
    def _():
        xe_ref[0:SUBLANES, :] = jnp.zeros((SUBLANES, LRU_WIDTH), F32)
        hc_ref[...] = jnp.zeros_like(hc_ref)

    xe_ref[SUBLANES:SUBLANES + tt, :] = xa_ref[...]
    u = cb_ref[...]
    for j in range(CONV_WIDTH):
        off = SUBLANES - (CONV_WIDTH - 1) + j
        u = u + xe_ref[off:off + tt, :] * cw_ref[j:j + 1, :]
    xe_ref[0:SUBLANES, :] = xa_ref[tt - SUBLANES:tt, :]

    sp = _softplus(-lam_ref[...])
    for n in range(LRU_BLOCKS):
        sl = slice(n * LRU_BLOCK, (n + 1) * LRU_BLOCK)
        un = u[:, sl]
        zz = _dot(un.astype(BF16), wg_ref[n])
        r = jax.nn.sigmoid(zz[:, :LRU_BLOCK] + ba_ref[:, sl])
        i = jax.nn.sigmoid(zz[:, LRU_BLOCK:] + bx_ref[:, sl])
        log_a = (-LRU_C) * r * sp[:, sl]
        a = jnp.exp(log_a)
        gain = jnp.sqrt(-jnp.tanh(log_a) * (a * a + 1.0))
        a_ref[:, sl] = a
        b_ref[:, sl] = gain * (i * un)

    row = lax.broadcasted_iota(jnp.int32, (SUBLANES, LRU_WIDTH), 0)

    def body(gi, carry):
        r0 = pl.multiple_of(gi * SUBLANES, SUBLANES)
        a = a_ref[pl.ds(r0, SUBLANES), :]
        b = b_ref[pl.ds(r0, SUBLANES), :]
        for s in (1, 2, 4):
            keep = row >= s
            a_sh = jnp.where(keep, pltpu.roll(a, s, 0), 1.0)
            b_sh = jnp.where(keep, pltpu.roll(b, s, 0), 0.0)
            b = b + a * b_sh
            a = a * a_sh
        hblk = a * carry + b
        b_ref[pl.ds(r0, SUBLANES), :] = hblk
        return hblk[SUBLANES - 1:SUBLANES, :]

    hc_ref[...] = lax.fori_loop(0, tt // SUBLANES, body, hc_ref[...])
    o_ref[...] = (b_ref[...] * _gelu_tanh(ga_ref[...])).astype(BF16)


def _lru_mixer(z, nb, cw, cb, wg, ba, bx, lam, tt=ROW_TILE):
    n = z.shape[0]
    nt = n // nb // tt
    w = LRU_WIDTH
    row = lambda c: pl.BlockSpec((1, w), lambda b, t: (0, 0))
    return pl.pallas_call(
        _lru_kernel,
        grid=(nb, nt),
        in_specs=[
            pl.BlockSpec((tt, w), lambda b, t: (b * nt + t, 0)),
            pl.BlockSpec((tt, w), lambda b, t: (b * nt + t, 1)),
            pl.BlockSpec((CONV_WIDTH, w), lambda b, t: (0, 0)),
            row(0),
            pl.BlockSpec((LRU_BLOCKS, LRU_BLOCK, 2 * LRU_BLOCK), lambda b, t: (0, 0, 0)),
            row(0), row(0), row(0),
        ],
        out_specs=pl.BlockSpec((tt, w), lambda b, t: (b * nt + t, 0)),
        out_shape=jax.ShapeDtypeStruct((n, w), BF16),
        scratch_shapes=[
            pltpu.VMEM((tt + SUBLANES, w), F32),
            pltpu.VMEM((tt, w), F32),
            pltpu.VMEM((tt, w), F32),
            pltpu.VMEM((1, w), F32),
        ],
        compiler_params=_cparams(("parallel", "arbitrary")),
        name="lru_mixer",
    )(z, z, cw, cb.reshape(1, w), wg, ba.reshape(1, w), bx.reshape(1, w), lam.reshape(1, w))


ATT_CB = 256


def _attn_prep_kernel(k_ref, v_ref, posk_ref, kaug_ref, vt_ref):
    e = DA_HEAD_DIM
    tk = k_ref.shape[0]
    ones = jnp.ones((SUBLANES, tk), F32)
    for hh in range(DA_HEADS):
        sl = slice(hh * e, (hh + 1) * e)
        kaug_ref[0, hh, :, 0:e] = k_ref[:, sl].astype(BF16)
        kaug_ref[0, hh, :, e:2 * e] = posk_ref[...]
        vt_ref[0, hh] = jnp.concatenate([v_ref[:, sl].T, ones], axis=0).astype(BF16)


def _attn_prep(z, nb, col0, tk):
    n = z.shape[0]
    tp = n // nb
    nt = tp // tk
    e = DA_HEAD_DIM
    kl = jnp.arange(tk)
    hi_lo = jnp.stack([(kl // LANES) * LANES, kl % LANES], axis=1).astype(F32)
    posk = jnp.pad(hi_lo, ((0, 0), (0, e - 2))).astype(BF16)
    ck, cv = (col0 + DA_WIDTH) // DA_WIDTH, (col0 + 2 * DA_WIDTH) // DA_WIDTH
    return pl.pallas_call(
        _attn_prep_kernel,
        grid=(nb, nt),
        in_specs=[pl.BlockSpec((tk, DA_WIDTH), lambda b, t: (b * nt + t, ck)),
                  pl.BlockSpec((tk, DA_WIDTH), lambda b, t: (b * nt + t, cv)),
                  pl.BlockSpec((tk, e), lambda b, t: (0, 0))],
        out_specs=[pl.BlockSpec((1, DA_HEADS, tk, 2 * e), lambda b, t: (b, 0, t, 0)),
                   pl.BlockSpec((1, DA_HEADS, e + SUBLANES, tk), lambda b, t: (b, 0, 0, t))],
        out_shape=[jax.ShapeDtypeStruct((nb, DA_HEADS, tp, 2 * e), BF16),
                   jax.ShapeDtypeStruct((nb, DA_HEADS, e + SUBLANES, tp), BF16)],
        compiler_params=_cparams(("parallel", "parallel")),
        name="attn_prep",
    )(z, z, posk)


def _attn_kernel(qi_tab, ki_tab, slopes_ref, q_ref, kaug_ref, vt_ref, lq1_ref, lk1_ref, lq2_ref,
                 lk2_ref, g_ref, o_ref, qaug_ref, m_ref, acc_ref, *, lambda_init):
    tq = q_ref.shape[0]
    tk = kaug_ref.shape[0]
    e = DA_HEAD_DIM
    ncb = 2 * tq // ATT_CB
    h = pl.program_id(1)
    p = pl.program_id(2)
    qi = qi_tab[p]
    ki = ki_tab[p]
    slope = slopes_ref[h]

    @pl.when(ki == 0)
    def _():
        q = q_ref[...] * (DA_QK_DIM ** -0.5)
        lane = lax.broadcasted_iota(jnp.int32, (tq, e), 1)
        aug = jnp.where(lane < 2, slope, 0.0)
        rows = [jnp.concatenate([jnp.where(sel, q, 0.0), aug], axis=1)
                for sel in (lane < DA_QK_DIM, lane >= DA_QK_DIM)]
        qboth = jnp.concatenate(rows, axis=0).astype(BF16)
        for cb in range(ncb):
            qaug_ref[cb] = qboth[cb * ATT_CB:(cb + 1) * ATT_CB]
        m_ref[...] = jnp.full_like(m_ref, NEG_BIG)
        acc_ref[...] = jnp.zeros_like(acc_ref)

    off = slope * jnp.full((1, ATT_CB), (ki - qi) * tk, jnp.int32).astype(F32)

    def run(masked):
        kaug = kaug_ref[...]
        v_t = vt_ref[...]

        ahead = 1
        pending = [_dot_nt(kaug, qaug_ref[cb]) for cb in range(ahead)]
        for cb in range(ncb):
            s = pending.pop(0)
            if cb + ahead < ncb:
                pending.append(_dot_nt(kaug, qaug_ref[cb + ahead]))
            if masked:
                col = cb * ATT_CB + lax.broadcasted_iota(jnp.int32, (1, ATT_CB), 1)
                ql = jnp.where(col >= tq, col - tq, col)
                kl = lax.broadcasted_iota(jnp.int32, (tk, 1), 0)
                s = jnp.where(kl <= ql, s, NEG_BIG)
            m_prev = m_ref[cb]
            m_new = jnp.maximum(m_prev, jnp.max(s, axis=0, keepdims=True) + off)
            alpha = jnp.exp(m_prev - m_new)
            pm = jnp.exp(s - (m_new - off)).astype(BF16)
            acc_ref[cb] = alpha * acc_ref[cb] + _dot(v_t, pm)
            m_ref[cb] = m_new

    @pl.when(ki != qi)
    def _():
        run(False)

    @pl.when(ki == qi)
    def _():
        run(True)
        lam = (jnp.exp(jnp.sum(lq1_ref[...] * lk1_ref[...], axis=-1, keepdims=True))
               - jnp.exp(jnp.sum(lq2_ref[...] * lk2_ref[...], axis=-1, keepdims=True)) + lambda_init)
        acc = jnp.concatenate([acc_ref[cb, 0:e] for cb in range(ncb)], axis=1)
        den = jnp.concatenate([acc_ref[cb, e:e + 1] for cb in range(ncb)], axis=1)
        o = acc / den
        o = o[:, :tq] - lam * o[:, tq:]
        ms = jnp.mean(o * o, axis=0, keepdims=True)
        o = o * lax.rsqrt(ms + 1e-5) * (g_ref[...] * (1.0 - lambda_init))
        o_ref[...] = o.T.astype(BF16)


def _diff_attention(z, nb, lq1, lk1, lq2, lk2, subln_g, lambda_init, col0, tq=ROW_TILE):
    n = z.shape[0]
    nq = n // nb // tq
    pairs = [(qi, ki) for qi in range(nq) for ki in range(qi + 1)]
    qi_tab = jnp.asarray([p[0] for p in pairs], jnp.int32)
    ki_tab = jnp.asarray([p[1] for p in pairs], jnp.int32)
    slopes = 2.0 ** (-8.0 * jnp.arange(1, DA_HEADS + 1, dtype=F32) / DA_HEADS)
    e = DA_HEAD_DIM
    assert (2 * tq) % ATT_CB == 0 and tq % LANES == 0
    kaug, vt = _attn_prep(z, nb, col0, tq)
    cq = col0 // e
    const = lambda shape: pl.BlockSpec(shape, lambda b, h, p, qt, kt: (0, 0))
    grid_spec = pltpu.PrefetchScalarGridSpec(
        num_scalar_prefetch=2,
        grid=(nb, DA_HEADS, len(pairs)),
        in_specs=[
            pl.BlockSpec(memory_space=pltpu.SMEM),
            pl.BlockSpec((tq, e), lambda b, h, p, qt, kt: (b * nq + qt[p], cq + h)),
            pl.BlockSpec((None, None, tq, 2 * e), lambda b, h, p, qt, kt: (b, h, kt[p], 0)),
            pl.BlockSpec((None, None, e + SUBLANES, tq), lambda b, h, p, qt, kt: (b, h, 0, kt[p])),
            const((1, DA_QK_DIM)), const((1, DA_QK_DIM)), const((1, DA_QK_DIM)), const((1, DA_QK_DIM)),
            const((e, 1)),
        ],
        out_specs=pl.BlockSpec((tq, e), lambda b, h, p, qt, kt: (b * nq + qt[p], h)),
        scratch_shapes=[
            pltpu.VMEM((2 * tq // ATT_CB, ATT_CB, 2 * e), BF16),
            pltpu.VMEM((2 * tq // ATT_CB, 1, ATT_CB), F32),
            pltpu.VMEM((2 * tq // ATT_CB, e + SUBLANES, ATT_CB), F32),
        ],
    )
    r64 = lambda a: a.reshape(1, DA_QK_DIM)
    return pl.pallas_call(
        functools.partial(_attn_kernel, lambda_init=lambda_init),
        grid_spec=grid_spec,
        out_shape=jax.ShapeDtypeStruct((n, DA_WIDTH), BF16),
        compiler_params=_cparams(("parallel", "parallel", "arbitrary")),
        name="diff_attention",
    )(qi_tab, ki_tab, slopes, z, kaug, vt, r64(lq1), r64(lk1), r64(lq2), r64(lk2), subln_g.reshape(e, 1))


def _s5_disc_kernel(lr_ref, li_ref, ldt_ref, ar_ref, ai_ref, cr_ref, ci_ref):
    lr, li = lr_ref[...], li_ref[...]
    dt = jnp.exp(ldt_ref[...])
    mag = jnp.exp(lr * dt)
    ar = mag * jnp.cos(li * dt)
    ai = mag * jnp.sin(li * dt)
    den = lr * lr + li * li
    ar_ref[...] = ar
    ai_ref[...] = ai
    cr_ref[...] = ((ar - 1.0) * lr + ai * li) / den
    ci_ref[...] = (ai * lr - (ar - 1.0) * li) / den


def _s5_discretise(lam_re, lam_im, log_dt):
    shp = jax.ShapeDtypeStruct(lam_re.shape, F32)
    ldt = jnp.broadcast_to(log_dt[:, None], lam_re.shape)
    return pl.pallas_call(_s5_disc_kernel, out_shape=(shp, shp, shp, shp), name="s5_discretise")(
        lam_re, lam_im, ldt)


S5_LC = 8
S5_KB = 8
S5_KB_CH = S5_WIDTH // S5_KB
S5_KB_ST = S5_GROUPS // S5_KB * S5_STATE


def _s5_chunk_kernel(u_ref, t_ref, e_ref, f_ref, ar_ref, ai_ref, y_ref, g_ref, pr_ref, pi_ref):
    tm = u_ref.shape[0] // S5_LC
    ns = S5_KB_ST

    @pl.when(pl.program_id(2) == 0)
    def _():
        g_ref[0:SUBLANES, :] = jnp.zeros((SUBLANES, 2 * ns), F32)
        ar, ai = ar_ref[0], ai_ref[0]
        for _ in range(int(math.log2(S5_LC))):
            ar, ai = ar * ar - ai * ai, 2.0 * ar * ai
        pr, pi = ar, ai
        for j in range(SUBLANES):
            pr_ref[j:j + 1, :] = pr
            pi_ref[j:j + 1, :] = pi
            pr, pi = pr * ar - pi * ai, pr * ai + pi * ar

    x = jnp.concatenate([u_ref[pl.ds(i, tm, stride=S5_LC), :] for i in range(S5_LC)],
                        axis=1).astype(BF16)
    g_ref[SUBLANES:SUBLANES + tm, :] = _dot(x, e_ref[0])
    row = lax.broadcasted_iota(jnp.int32, (SUBLANES, ns), 0)

    def body(gi, carry):
        cr, ci = carry
        r0 = pl.multiple_of(SUBLANES + gi * SUBLANES, SUBLANES)
        xr = g_ref[pl.ds(r0, SUBLANES), 0:ns]
        xi = g_ref[pl.ds(r0, SUBLANES), ns:2 * ns]
        for s in (1, 2, 4):
            keep = row >= s
            ar = pr_ref[s - 1:s, :]
            ai = pi_ref[s - 1:s, :]
            sr = jnp.where(keep, pltpu.roll(xr, s, 0), 0.0)
            si = jnp.where(keep, pltpu.roll(xi, s, 0), 0.0)
            xr, xi = xr + ar * sr - ai * si, xi + ar * si + ai * sr
        pr, pi = pr_ref[...], pi_ref[...]
        xr, xi = xr + pr * cr - pi * ci, xi + pr * ci + pi * cr
        g_ref[pl.ds(r0, SUBLANES), 0:ns] = xr
        g_ref[pl.ds(r0, SUBLANES), ns:2 * ns] = xi
        return xr[SUBLANES - 1:SUBLANES, :], xi[SUBLANES - 1:SUBLANES, :]

    carry_in = (g_ref[SUBLANES - 1:SUBLANES, 0:ns], g_ref[SUBLANES - 1:SUBLANES, ns:2 * ns])
    lax.fori_loop(0, tm // SUBLANES, body, carry_in)
    h_start = g_ref[SUBLANES - 1:SUBLANES - 1 + tm, :].astype(BF16)
    y = _dot(x, t_ref[0]) + _dot(h_start, f_ref[0])
    for j in range(S5_LC):
        y_ref[pl.ds(j, tm, stride=S5_LC), :] = y[:, j * S5_KB_CH:(j + 1) * S5_KB_CH]
    g_ref[SUBLANES - 1:SUBLANES, :] = g_ref[SUBLANES - 1 + tm:SUBLANES + tm, :]


def _s5_chunked(u, nb, t_w, e_w, f_w, ar, ai, tm=208):
    n = u.shape[0]
    rows = n // S5_LC
    nt = rows // nb // tm
    assert nt * tm * nb == rows and tm % SUBLANES == 0 and S5_KB_CH == LANES
    kw = S5_LC * S5_KB_CH
    tile = pl.BlockSpec((tm * S5_LC, S5_KB_CH), lambda kb, b, t: (b * nt + t, kb))
    wspec = lambda k, m: pl.BlockSpec((1, k, m), lambda kb, b, t: (kb, 0, 0), pipeline_mode=pl.Buffered(1))
    aspec = pl.BlockSpec((1, 1, S5_KB_ST), lambda kb, b, t: (kb, 0, 0))
    return pl.pallas_call(
        _s5_chunk_kernel,
        grid=(S5_KB, nb, nt),
        in_specs=[tile, wspec(kw, kw), wspec(kw, 2 * S5_KB_ST), wspec(2 * S5_KB_ST, kw), aspec, aspec],
        out_specs=tile,
        out_shape=jax.ShapeDtypeStruct((n, S5_WIDTH), F32),
        scratch_shapes=[
            pltpu.VMEM((tm + SUBLANES, 2 * S5_KB_ST), F32),
            pltpu.VMEM((SUBLANES, S5_KB_ST), F32),
            pltpu.VMEM((SUBLANES, S5_KB_ST), F32),
        ],
        compiler_params=_cparams(("parallel", "parallel", "arbitrary")),
        name="s5_chunk",
    )(u, t_w, e_w, f_w, ar.reshape(S5_KB, 1, S5_KB_ST), ai.reshape(S5_KB, 1, S5_KB_ST))


def _s5_post_kernel(y_ref, u_ref, d_ref, gw_ref, gb_ref, o_ref):
    y = _gelu_tanh(y_ref[...] + d_ref[...] * u_ref[...])
    gate = jax.nn.sigmoid(_dot(y.astype(BF16), gw_ref[...]) + gb_ref[...])
    o_ref[...] = (y * gate).astype(BF16)


def _s5_post(y, u, d_skip, glu_w, glu_b):
    n, w = u.shape
    tm = ROW_TILE
    rows = pl.BlockSpec((tm, w), lambda i: (i, 0))
    full = lambda shape: pl.BlockSpec(shape, lambda i: (0,) * len(shape))
    return pl.pallas_call(
        _s5_post_kernel,
        grid=(n // tm,),
        in_specs=[rows, rows, full((1, w)), full((w, w)), full((1, w))],
        out_specs=rows,
        out_shape=jax.ShapeDtypeStruct((n, w), BF16),
        compiler_params=_cparams(("parallel",)),
        name="s5_post",
    )(y, u, d_skip.reshape(1, w), glu_w, glu_b.reshape(1, w))


def _token_shift(x_ref, mu_ref, xe_ref, first):
    tt = x_ref.shape[0]

    @pl.when(first)
    def _():
        xe_ref[0:SUBLANES, :] = jnp.zeros((SUBLANES, x_ref.shape[1]), F32)

    x = x_ref[...]
    xe_ref[SUBLANES:SUBLANES + tt, :] = x
    xprev = xe_ref[SUBLANES - 1:SUBLANES - 1 + tt, :]
    xe_ref[0:SUBLANES, :] = x_ref[tt - SUBLANES:tt, :]
    return x + (xprev - x) * mu_ref[...]


def _rw_prep_kernel(xr_ref, xk_ref, xv_ref, xl_ref, mur_ref, muk_ref, muv_ref, mul_ref,
                    w0_ref, w2_ref, a0_ref, a2_ref, g2_ref, kk_ref, ka_ref, ones_ref,
                    r_out, lw_out, k_out, v_out, kk_out, a_out, g_out,
                    er_ref, ek_ref, ev_ref, el_ref):
    first = pl.program_id(1) == 0
    r = _token_shift(xr_ref, mur_ref, er_ref, first)
    k = _token_shift(xk_ref, muk_ref, ek_ref, first)
    v = _token_shift(xv_ref, muv_ref, ev_ref, first)
    lo = _token_shift(xl_ref, mul_ref, el_ref, first)
    wl = lo[:, 0:RW_LORA_PAD]
    al = lo[:, RW_LORA_PAD:2 * RW_LORA_PAD]
    gl = lo[:, 2 * RW_LORA_PAD:2 * RW_LORA_PAD + RW_GATE_LORA]

    wdec = -_softplus(-(w0_ref[...] + _dot(jnp.tanh(wl).astype(BF16), w2_ref[...]))) - 0.5
    a = jax.nn.sigmoid(a0_ref[...] + _dot(al.astype(BF16), a2_ref[...]))
    g = _dot(jax.nn.sigmoid(gl).astype(BF16), g2_ref[...])
    kkr = k * kk_ref[...]
    ssq = _split_dot(kkr * kkr, ones_ref[...])
    kk = kkr / jnp.maximum(jnp.sqrt(ssq), 1e-12)

    r_out[...] = r
    lw_out[...] = -jnp.exp(wdec)
    k_out[...] = k * (1.0 + (a - 1.0) * ka_ref[...])
    v_out[...] = v
    kk_out[...] = kk
    a_out[...] = a
    g_out[...] = g


RW_LORA_COLS = 2 * RW_LORA_PAD + RW_GATE_LORA


def _rw_prep(z, nb, col0, mu, w0, w2, a0, a2, g2, k_k, k_a, ones_bd, tt=320):
    n = z.shape[0]
    nt = n // nb // tt
    w = RW_WIDTH
    lw = RW_LORA_COLS
    assert col0 % w == 0 and (col0 + 3 * w) % lw == 0
    cb = col0 // w
    cl = (col0 + 3 * w) // lw
    full = lambda shape: pl.BlockSpec(shape, lambda b, t: (0,) * len(shape))
    xblk = lambda width, c: pl.BlockSpec((tt, width), lambda b, t: (b * nt + t, c))
    out = jax.ShapeDtypeStruct((n, w), F32)
    ospec = pl.BlockSpec((tt, w), lambda b, t: (b * nt + t, 0))
    mu = mu.reshape(1, 3 * w + lw)
    return pl.pallas_call(
        _rw_prep_kernel,
        grid=(nb, nt),
        in_specs=[
            xblk(w, cb), xblk(w, cb + 1), xblk(w, cb + 2), xblk(lw, cl),
            full((1, w)), full((1, w)), full((1, w)), full((1, lw)),
            full((1, w)), full((RW_LORA_PAD, w)), full((1, w)), full((RW_LORA_PAD, w)),
            full((RW_GATE_LORA, w)), full((1, w)), full((1, w)), full((w, w)),
        ],
        out_specs=[ospec] * 7,
        out_shape=[out] * 7,
        scratch_shapes=[pltpu.VMEM((tt + SUBLANES, w), F32)] * 3 + [pltpu.VMEM((tt + SUBLANES, lw), F32)],
        compiler_params=_cparams(("parallel", "arbitrary")),
        name="rwkv_prep",
    )(z, z, z, z, mu[:, 0:w], mu[:, w:2 * w], mu[:, 2 * w:3 * w], mu[:, 3 * w:],
      w0.reshape(1, w), w2, a0.reshape(1, w), a2, g2, k_k.reshape(1, w), k_a.reshape(1, w), ones_bd)


def _rw_chunk_kernel(r_ref, lw_ref, k_ref, v_ref, kk_ref, a_ref, g_ref, rk_ref, lnw_ref, lnb_ref,
                     ones_ref, o_ref, s_ref):
    @pl.when(pl.program_id(1) == 0)
    def _():
        s_ref[...] = jnp.zeros_like(s_ref)

    gw = RW_GROUP * RW_HEAD_DIM
    nb, _, width = r_ref.shape
    chains = [(b, slice(j * gw, (j + 1) * gw)) for b in range(nb) for j in range(width // gw)]
    ins = [tuple(ref[b, :, sl] for ref in (r_ref, lw_ref, k_ref, v_ref, kk_ref, a_ref))
           for b, sl in chains]
    s0s = [s_ref[i] for i in range(len(chains))]
    ys, s_news = _rw_chunk_steps(ins, s0s)
    for i, (b, sl) in enumerate(chains):
        s_ref[i] = s_news[i]
        o_ref[b, :, sl] = _rw_output(ys[i], r_ref[b, :, sl], k_ref[b, :, sl], v_ref[b, :, sl],
                                     g_ref[b, :, sl], rk_ref[:, sl], lnw_ref[:, sl], lnb_ref[:, sl],
                                     ones_ref[...])


def _rw_chunk_prepare(r, lw, k, v, kk, a):
    L = RW_CHUNK
    gw = RW_GROUP * RW_HEAD_DIM

    ti = lax.broadcasted_iota(jnp.int32, (L, L), 0)
    tj = lax.broadcasted_iota(jnp.int32, (L, L), 1)
    tri = jnp.where(ti >= tj, 1.0, 0.0).astype(BF16)
    x1 = lw.astype(BF16)
    r1 = lw - x1.astype(F32)
    x2 = r1.astype(BF16)
    x3 = (r1 - x2.astype(F32)).astype(BF16)
    cl = _dot(tri, x1) + _dot(tri, x2) + _dot(tri, x3)
    g_last = jnp.exp(cl[L - 1:L, :])
    e_in = jnp.exp(cl)
    e_ex = jnp.exp(cl - lw)
    e_inv = jnp.exp(-cl)
    beta = kk * a
    a_t = -kk * e_ex
    r_t = r * e_in
    b_t = beta * e_inv
    k_t = k * e_inv

    lane_head = lax.shift_right_logical(lax.broadcasted_iota(jnp.int32, (L, gw), 1),
                                        int(math.log2(RW_HEAD_DIM)))

    def stack(x):
        return jnp.concatenate([jnp.where(lane_head == hh, x, 0.0) for hh in range(RW_GROUP)],
                               axis=0).astype(BF16)

    ar_s = jnp.concatenate([stack(a_t), stack(r_t)], axis=0)
    bk_s = jnp.concatenate([stack(b_t), stack(k_t)], axis=0)
    return dict(ar_s=ar_s, bk_s=bk_s, v_s=stack(v), bh_s=stack(b_t * g_last), kh_s=stack(k_t * g_last),
                g_last=g_last)


def _rw_chunk_steps(ins, s0s):
    L = RW_CHUNK
    rows = RW_GROUP * L
    n = range(len(ins))
    pre = [_rw_chunk_prepare(*x) for x in ins]
    prod = [_dot_nt(p["ar_s"], p["bk_s"]) for p in pre]
    w0 = [_dot_nt(p["ar_s"], s0.astype(BF16)) for p, s0 in zip(pre, s0s)]
    ri = lax.broadcasted_iota(jnp.int32, (rows, rows), 0)
    ci = lax.broadcasted_iota(jnp.int32, (rows, rows), 1)
    strict = ri > ci
    incl = ri >= ci
    npow = [jnp.where(strict, pr[:rows, :rows], 0.0).astype(BF16) for pr in prod]
    a_ak = [jnp.where(strict, pr[:rows, rows:], 0.0).astype(BF16) for pr in prod]
    m_rb = [jnp.where(incl, pr[rows:, :rows], 0.0).astype(BF16) for pr in prod]
    m_rk = [jnp.where(incl, pr[rows:, rows:], 0.0).astype(BF16) for pr in prod]
    x = [w0[i][:rows] + _dot(a_ak[i], pre[i]["v_s"]) for i in n]
    steps = int(math.log2(L))
    for kx in range(steps):
        x = [x[i] + _dot(npow[i], x[i].astype(BF16)) for i in n]
        if kx + 1 < steps:
            npow = [_dot(npow[i], npow[i]).astype(BF16) for i in n]
    u_s = [xi.astype(BF16) for xi in x]
    y_s = [w0[i][rows:] + _dot(m_rb[i], u_s[i]) + _dot(m_rk[i], pre[i]["v_s"]) for i in n]
    ys = []
    for ysi in y_s:
        y = ysi[0:L]
        for hh in range(1, RW_GROUP):
            y = y + ysi[hh * L:(hh + 1) * L]
        ys.append(y)
    s_new = [s0s[i] * pre[i]["g_last"] + _dot_tn(u_s[i], pre[i]["bh_s"])
             + _dot_tn(pre[i]["v_s"], pre[i]["kh_s"]) for i in n]
    return ys, s_new


def _rw_output(y, r, k, v, g, r_k, ln_w, ln_b, ones):
    inv_n = 1.0 / RW_HEAD_DIM
    mean = _split_dot(y, ones) * inv_n
    yc = y - mean
    var = _split_dot(yc * yc, ones) * inv_n
    yn = yc * lax.rsqrt(var + RW_GN_EPS) * ln_w + ln_b
    bonus = _split_dot(r * k * r_k, ones) * v
    return ((yn + bonus) * g).astype(BF16)


def _rw_chunk(prep, nb, r_k, ln_w, ln_b, ones_g):
    n = prep[0].shape[0]
    L = RW_CHUNK
    gw = RW_GROUP * RW_HEAD_DIM
    bw = RW_STEP_GROUPS * gw
    tp = n // nb
    w = RW_WIDTH
    blk = lambda: pl.BlockSpec((nb, L, bw), lambda gi, c: (0, c, gi))
    par = lambda: pl.BlockSpec((1, bw), lambda gi, c: (0, gi))
    out = pl.pallas_call(
        _rw_chunk_kernel,
        grid=(w // bw, tp // L),
        in_specs=[blk() for _ in range(7)] + [par(), par(), par(),
                                             pl.BlockSpec((gw, gw), lambda gi, c: (0, 0))],
        out_specs=blk(),
        out_shape=jax.ShapeDtypeStruct((nb, tp, w), BF16),
        scratch_shapes=[pltpu.VMEM((nb * RW_STEP_GROUPS, gw, gw), F32)],
        compiler_params=_cparams(("parallel", "arbitrary")),
        name="rwkv_chunk",
    )(*[x.reshape(nb, tp, w) for x in prep], r_k.reshape(1, w), ln_w.reshape(1, w), ln_b.reshape(1, w),
      ones_g)
    return out.reshape(n, w)


def _head_ones(width, head):
    idx = jnp.arange(width) // head
    return (idx[:, None] == idx[None, :]).astype(BF16)


def _pack_lru_gates(wa, wx):
    return jnp.concatenate([wa, wx], axis=-1).astype(BF16)


def _pack_odd_in(w_in, mu):
    s5 = S5_WIDTH
    c_rkv = s5 + 3 * RW_WIDTH
    c_wl = c_rkv + RW_DECAY_LORA
    c_al = c_wl + RW_AAA_LORA
    padw = lambda x, n: jnp.pad(x, ((0, 0), (0, n - x.shape[1])))
    w = jnp.concatenate([
        w_in[:, :c_rkv],
        padw(w_in[:, c_rkv:c_wl], RW_LORA_PAD),
        padw(w_in[:, c_wl:c_al], RW_LORA_PAD),
        w_in[:, c_al:],
    ], axis=1)
    m = mu[None, :]
    o = s5
    mu_p = jnp.concatenate([
        m[:, :c_rkv - o],
        padw(m[:, c_rkv - o:c_wl - o], RW_LORA_PAD),
        padw(m[:, c_wl - o:c_al - o], RW_LORA_PAD),
        m[:, c_al - o:],
    ], axis=1)[0]
    return w.astype(BF16), mu_p


def _pad_rows(x, n):
    return jnp.pad(x, ((0, n - x.shape[0]), (0, 0)))


def _pack_s5(ar, ai, cr, ci, b_re, b_im, c_re, c_im):
    lc, nk, gpb = S5_LC, S5_KB, S5_GROUPS // S5_KB
    hp = lax.Precision.HIGHEST
    bbr = cr[..., None] * b_re - ci[..., None] * b_im
    bbi = cr[..., None] * b_im + ci[..., None] * b_re
    pr, pi = [jnp.ones_like(ar)], [jnp.zeros_like(ai)]
    for _ in range(lc):
        pr, pi = pr + [pr[-1] * ar - pi[-1] * ai], pi + [pr[-1] * ai + pi[-1] * ar]
    pr, pi = jnp.stack(pr), jnp.stack(pi)
    car = c_re[None] * pr[:, :, None, :] - c_im[None] * pi[:, :, None, :]
    cai = c_re[None] * pi[:, :, None, :] + c_im[None] * pr[:, :, None, :]
    kt = (jnp.einsum("tgcp,gpd->tgcd", car[:lc], bbr, precision=hp)
          - jnp.einsum("tgcp,gpd->tgcd", cai[:lc], bbi, precision=hp))
    kt = jnp.concatenate([kt, jnp.zeros_like(kt[:1])], axis=0)
    pos = jnp.arange(lc)
    lag = pos[None, :] - pos[:, None]
    lag = jnp.where(lag >= 0, lag, lc)
    kw = lc * S5_KB_CH

    def widen(x):
        *lead, g, a, m = x.shape
        rep = jnp.tile(jnp.eye(m, dtype=BF16), (1, gpb))
        y = jnp.dot(x.astype(BF16).reshape(-1, m), rep).reshape(*lead, g * a, gpb * m)
        keep = (jnp.arange(g * a)[:, None] // a) == (jnp.arange(gpb * m)[None, :] // m)
        return jnp.where(keep, y, jnp.zeros((), BF16))

    blocks = widen(kt.reshape(lc + 1, nk, gpb, S5_GROUP, S5_GROUP).transpose(0, 1, 2, 4, 3))
    t_w = blocks[lag].transpose(2, 0, 3, 1, 4).reshape(nk, kw, kw)
    rev = lc - 1 - pos
    er = pr[rev][..., None] * bbr[None] - pi[rev][..., None] * bbi[None]
    ei = pr[rev][..., None] * bbi[None] + pi[rev][..., None] * bbr[None]

    def e_part(x):
        x = x.reshape(lc, nk, gpb, S5_STATE, S5_GROUP).transpose(1, 0, 2, 4, 3)
        return widen(x).reshape(nk, kw, S5_KB_ST)

    def f_part(x):
        x = x.reshape(lc, nk, gpb, S5_GROUP, S5_STATE).transpose(1, 0, 2, 4, 3)
        return widen(x).transpose(0, 2, 1, 3).reshape(nk, S5_KB_ST, kw)

    e_w = jnp.concatenate([e_part(er), e_part(ei)], axis=2)
    f_w = jnp.concatenate([f_part(car[1:]), f_part(-cai[1:])], axis=1)
    return t_w, e_w, f_w


def kernel(x, meta_tokens, norm_mix_g, norm_ffn_g, final_norm_g, ev_w_in, ev_conv_w, ev_conv_b, ev_lru_wa, ev_lru_ba, ev_lru_wx, ev_lru_bx, ev_lru_lambda, ev_lq1, ev_lk1, ev_lq2, ev_lk2, ev_subln_g, ev_w_out, od_w_in, od_s5_lam_re, od_s5_lam_im, od_s5_log_dt, od_s5_b_re, od_s5_b_im, od_s5_c_re, od_s5_c_im, od_s5_d, od_glu_w, od_glu_b, od_rw_mu, od_rw_w0, od_rw_w2, od_rw_a0, od_rw_a2, od_rw_g2, od_rw_kk, od_rw_ka, od_rw_rk, od_rw_ln_w, od_rw_ln_b, od_w_out, ffn_w_gate, ffn_w_up, ffn_w_down):
    nb, seq, d = x.shape
    depth = norm_mix_g.shape[0]
    t_real = N_META + seq
    tp = -(-t_real // ROW_TILE) * ROW_TILE
    meta = jnp.broadcast_to(meta_tokens[None].astype(x.dtype), (nb, N_META, d))
    pad = jnp.zeros((nb, tp - t_real, d), x.dtype)
    h = jnp.concatenate([meta, x, pad], axis=1).reshape(nb * tp, d)

    ones_rw = _head_ones(RW_WIDTH, RW_HEAD_DIM)
    ones_grp = _head_ones(RW_GROUP * RW_HEAD_DIM, RW_HEAD_DIM)

    ffn_wg_b, ffn_wu_b, ffn_wd_b = (w.astype(BF16) for w in (ffn_w_gate, ffn_w_up, ffn_w_down))
    ev_w_out_b, od_w_out_b = ev_w_out.astype(BF16), od_w_out.astype(BF16)

    for layer in range(depth):
        j = layer // 2
        if layer % 2 == 0:
            lambda_init = 0.8 - 0.6 * math.exp(-0.3 * layer)
            z = _rms_matmul(h, norm_mix_g[layer], ev_w_in[j].astype(BF16), tn=1280)
            ya = _lru_mixer(z, nb, ev_conv_w[j], ev_conv_b[j], _pack_lru_gates(ev_lru_wa[j], ev_lru_wx[j]),
                            ev_lru_ba[j], ev_lru_bx[j], ev_lru_lambda[j])
            yb = _diff_attention(z, nb, ev_lq1[j], ev_lk1[j], ev_lq2[j], ev_lk2[j], ev_subln_g[j],
                                 lambda_init, col0=2 * LRU_WIDTH)
            mix_a, mix_b, w_out = ya, yb, ev_w_out_b
        else:
            w_in, mu_p = _pack_odd_in(od_w_in[j], od_rw_mu[j])
            u = _rms_matmul(h, norm_mix_g[layer], w_in[:, :S5_WIDTH], tn=S5_WIDTH)
            z = _rms_matmul(h, norm_mix_g[layer], w_in[:, S5_WIDTH:], tn=(w_in.shape[1] - S5_WIDTH) // 2)
            ar, ai, cr, ci = _s5_discretise(od_s5_lam_re[j], od_s5_lam_im[j], od_s5_log_dt[j])
            t_w, e_w, f_w = _pack_s5(ar, ai, cr, ci, od_s5_b_re[j], od_s5_b_im[j], od_s5_c_re[j],
                                     od_s5_c_im[j])
            y8 = _s5_chunked(u, nb, t_w, e_w, f_w, ar, ai)
            yc = _s5_post(y8, u, od_s5_d[j], od_glu_w[j].astype(BF16), od_glu_b[j])
            prep = _rw_prep(z, nb, 0, mu_p, od_rw_w0[j],
                            _pad_rows(od_rw_w2[j], RW_LORA_PAD).astype(BF16), od_rw_a0[j],
                            _pad_rows(od_rw_a2[j], RW_LORA_PAD).astype(BF16), od_rw_g2[j].astype(BF16),
                            od_rw_kk[j], od_rw_ka[j], ones_rw)
            yd = _rw_chunk(prep, nb, od_rw_rk[j], od_rw_ln_w[j], od_rw_ln_b[j], ones_grp)
            mix_a, mix_b, w_out = yc, yd, od_w_out_b
        h = _mix_ffn(h, mix_a, mix_b, w_out, j, norm_ffn_g[layer], ffn_wg_b, ffn_wu_b, ffn_wd_b, layer)
    return _final_norm(h, final_norm_g, nb, seq)
```
